```python
import jax, jax.numpy as jnp
from jax import lax
import numpy as np


D_MODEL = 2048
BATCH = 4
SEQ = 4096
DEPTH = 2

GRID_W = 64
CTX_LEN = 256
N_HEADS = 8
N_KV_HEADS = 2
HEAD_DIM = 128
ATTN_W = N_HEADS * HEAD_DIM
KV_W = N_KV_HEADS * HEAD_DIM
Q_BLOCK = 128
ROPE_THETA = 10000.0
ROPE_PAIRS = HEAD_DIM // 4
D_RNN = 1024
RNN_BLOCKS = 8
RNN_BLOCK_W = D_RNN // RNN_BLOCKS
CONV_W = 4
RG_C = 8.0
AR_IN = ATTN_W + 2 * KV_W + 2 * D_RNN
AR_OUT = ATTN_W + D_RNN
D_GM = 2048
GM_GROUPS = 16
GM_GROUP_W = D_GM // GM_GROUPS
CHUNK = 128
D_FF = 4 * D_MODEL
EPS = 1e-6
N_EVEN = (DEPTH + 1) // 2
N_ODD = DEPTH // 2

kernel_name = 'hybrid_attn_rglru_chunkgmlp_diffusion'


def rms_norm(x, g):
    xf = x.astype(jnp.float32)
    y = xf * lax.rsqrt(jnp.mean(xf * xf, axis=-1, keepdims=True) + EPS)
    return (y * g.astype(jnp.float32)).astype(x.dtype)


def layer_norm(x, g, b):
    xf = x.astype(jnp.float32)
    mu = jnp.mean(xf, axis=-1, keepdims=True)
    xc = xf - mu
    y = xc * lax.rsqrt(jnp.mean(xc * xc, axis=-1, keepdims=True) + EPS)
    return (y * g.astype(jnp.float32) + b.astype(jnp.float32)).astype(x.dtype)


def modulate(h, shift, scale):
    return h * (1 + scale) + shift


def axial_angles(n):
    rows = n // GRID_W
    r_idx, c_idx = jnp.meshgrid(jnp.arange(rows), jnp.arange(GRID_W), indexing='ij')
    r_idx = r_idx.reshape(-1).astype(jnp.float32)
    c_idx = c_idx.reshape(-1).astype(jnp.float32)
    freqs = ROPE_THETA ** (-jnp.arange(ROPE_PAIRS, dtype=jnp.float32) / ROPE_PAIRS)
    return r_idx[:, None] * freqs, c_idx[:, None] * freqs


def rope_1d(x, ang):
    x1, x2 = jnp.split(x.astype(jnp.float32), 2, axis=-1)
    cos = jnp.cos(ang)[None, :, None, :]
    sin = jnp.sin(ang)[None, :, None, :]
    return jnp.concatenate([x1 * cos - x2 * sin, x2 * cos + x1 * sin], axis=-1)


def rope_2d(x, ang_row, ang_col):
    half = HEAD_DIM // 2
    out = jnp.concatenate([rope_1d(x[..., :half], ang_row), rope_1d(x[..., half:], ang_col)], axis=-1)
    return out.astype(x.dtype)


def gqa_attend(q, k, v):
    bsz, n = q.shape[0], q.shape[1]
    nb = n // Q_BLOCK
    groups = N_HEADS // N_KV_HEADS
    scale = HEAD_DIM ** -0.5
    qb = q.reshape(bsz, nb, Q_BLOCK, N_KV_HEADS, groups, HEAD_DIM).transpose(1, 0, 2, 3, 4, 5)

    def one_block(qi):
        s = jnp.einsum('bqkgd,btkd->bkgqt', qi, k).astype(jnp.float32) * scale
        p = jax.nn.softmax(s, axis=-1).astype(v.dtype)
        return jnp.einsum('bkgqt,btkd->bqkgd', p, v)

    o = lax.map(one_block, qb)
    return o.transpose(1, 0, 2, 3, 4, 5).reshape(bsz, n, N_HEADS * HEAD_DIM)


def centred_dwconv(x, w, b):
    n = x.shape[1]
    left = CONV_W // 2
    xp = jnp.pad(x, ((0, 0), (left, CONV_W - 1 - left), (0, 0)))
    y = b
    for j in range(CONV_W):
        y = y + xp[:, j:j + n] * w[j]
    return y


def block_diag(x, w, b):
    xb = x.reshape(x.shape[0], x.shape[1], RNN_BLOCKS, RNN_BLOCK_W)
    return jnp.einsum('bsnc,ncd->bsnd', xb, w).reshape(x.shape) + b


def rglru_direction(x, wa, ba, wx, bx, lam, h0, reverse):
    r = jax.nn.sigmoid(block_diag(x, wa, ba).astype(jnp.float32))
    i = jax.nn.sigmoid(block_diag(x, wx, bx).astype(jnp.float32))
    log_a = -RG_C * r * jax.nn.softplus(-lam.astype(jnp.float32))
    a = jnp.exp(log_a)
    b = jnp.sqrt(-jnp.expm1(2.0 * log_a)) * (i * x.astype(jnp.float32))

    def combine(lhs, rhs):
        return (lhs[0] * rhs[0], rhs[0] * lhs[1] + rhs[1])

    a_cum, h = lax.associative_scan(combine, (a, b), reverse=reverse, axis=1)
    h = h + a_cum * h0[:, None, :]
    final = h[:, 0] if reverse else h[:, -1]
    return h, final


def mix_attn_rglru(hl, hc, w_in, q_g, k_g, conv_w, conv_b, wa, ba, wx, bx, lam, w_out, need_ctx):
    s1 = ATTN_W
    s2 = s1 + KV_W
    s3 = s2 + KV_W
    s4 = s3 + D_RNN

    def project(h):
        bsz, n = h.shape[0], h.shape[1]
        q, k, v, xr, gr = jnp.split(h @ w_in, [s1, s2, s3, s4], axis=-1)
        q = rms_norm(q.reshape(bsz, n, N_HEADS, HEAD_DIM), q_g)
        k = rms_norm(k.reshape(bsz, n, N_KV_HEADS, HEAD_DIM), k_g)
        v = v.reshape(bsz, n, N_KV_HEADS, HEAD_DIM)
        xr = centred_dwconv(xr, conv_w, conv_b)
        return q, k, v, xr, gr

    ql, kl, vl, xl, gl = project(hl)
    qc, kc, vc, xc, gc = project(hc)

    ang_r, ang_c = axial_angles(hl.shape[1])
    ql = rope_2d(ql, ang_r, ang_c)
    kl = rope_2d(kl, ang_r, ang_c)
    k_all = jnp.concatenate([kc, kl], axis=1)
    v_all = jnp.concatenate([vc, vl], axis=1)
    attn_l = gqa_attend(ql, k_all, v_all)

    zeros = jnp.zeros((hc.shape[0], D_RNN), jnp.float32)
    hcf, fin_f = rglru_direction(xc, wa[0], ba[0], wx[0], bx[0], lam[0], zeros, False)
    hcb, fin_b = rglru_direction(xc, wa[1], ba[1], wx[1], bx[1], lam[1], zeros, True)
    hlf, _ = rglru_direction(xl, wa[0], ba[0], wx[0], bx[0], lam[0], fin_f, False)
    hlb, _ = rglru_direction(xl, wa[1], ba[1], wx[1], bx[1], lam[1], fin_b, True)
    rnn_l = ((hlf + hlb) * jax.nn.gelu(gl.astype(jnp.float32))).astype(hl.dtype)
    out_l = jnp.concatenate([attn_l, rnn_l], axis=-1) @ w_out

    out_c = None
    if need_ctx:
        attn_c = gqa_attend(qc, kc, vc)
        rnn_c = ((hcf + hcb) * jax.nn.gelu(gc.astype(jnp.float32))).astype(hc.dtype)
        out_c = jnp.concatenate([attn_c, rnn_c], axis=-1) @ w_out
    return out_l, out_c


def chunk_gmlp(h, w_in, b_in, v_g, v_b, w_sp, b_sp, w_out):
    bsz, n = h.shape[0], h.shape[1]
    z = jax.nn.gelu(h @ w_in + b_in)
    u, v = jnp.split(z, 2, axis=-1)
    v = layer_norm(v, v_g, v_b)
    v = v.reshape(bsz, n // CHUNK, CHUNK, GM_GROUPS, GM_GROUP_W)
    sv = jnp.einsum('gpq,bcqgd->bcpgd', w_sp, v) + b_sp.T[None, None, :, :, None]
    return (u * sv.reshape(bsz, n, D_GM)) @ w_out


def sq_relu_mlp(h, w1, w2):
    return jnp.square(jax.nn.relu(h @ w1)) @ w2


def setup_inputs(seed: int = 0) -> dict:
    key = jax.random.key(seed)
    ks = jax.random.split(key, 32)
    f32 = jnp.float32
    D = D_MODEL

    def nrm(k, shape, scale):
        return jax.random.normal(k, shape, f32) * scale

    lam_u = jax.random.uniform(ks[20], (N_EVEN, 2, D_RNN), f32, 0.9, 0.999)
    a0 = lam_u ** (1.0 / RG_C)
    return {
        'x': nrm(ks[0], (BATCH, SEQ, D), 1.0),
        'c': nrm(ks[1], (BATCH, D), 1.0),
        'ctx': nrm(ks[2], (BATCH, CTX_LEN, D), 1.0),
        'c_ctx': nrm(ks[3], (D,), 1.0),
        'w_mod': nrm(ks[4], (DEPTH, D, 6 * D), 0.5 * D ** -0.5),
        'b_mod': nrm(ks[5], (DEPTH, 6 * D), 0.02),
        'norm_g': 1.0 + nrm(ks[6], (DEPTH, 4, D), 0.02),
        'w_ff_in': nrm(ks[7], (DEPTH, D, D_FF), D ** -0.5),
        'w_ff_out': nrm(ks[8], (DEPTH, D_FF, D), D_FF ** -0.5),
        'ar_w_in': nrm(ks[9], (N_EVEN, D, AR_IN), D ** -0.5),
        'ar_q_g': 1.0 + nrm(ks[10], (N_EVEN, HEAD_DIM), 0.02),
        'ar_k_g': 1.0 + nrm(ks[11], (N_EVEN, HEAD_DIM), 0.02),
        'ar_conv_w': nrm(ks[12], (N_EVEN, CONV_W, D_RNN), CONV_W ** -0.5),
        'ar_conv_b': nrm(ks[13], (N_EVEN, D_RNN), 0.02),
        'ar_wa': nrm(ks[14], (N_EVEN, 2, RNN_BLOCKS, RNN_BLOCK_W, RNN_BLOCK_W), RNN_BLOCK_W ** -0.5),
        'ar_ba': nrm(ks[15], (N_EVEN, 2, D_RNN), 0.02),
        'ar_wx': nrm(ks[16], (N_EVEN, 2, RNN_BLOCKS, RNN_BLOCK_W, RNN_BLOCK_W), RNN_BLOCK_W ** -0.5),
        'ar_bx': nrm(ks[17], (N_EVEN, 2, D_RNN), 0.02),
        'ar_lambda': jnp.log(a0) - jnp.log1p(-a0),
        'ar_w_out': nrm(ks[18], (N_EVEN, AR_OUT, D), AR_OUT ** -0.5),
        'gm_w_in': nrm(ks[19], (N_ODD, D, 2 * D_GM), D ** -0.5),
        'gm_b_in': nrm(ks[21], (N_ODD, 2 * D_GM), 0.02),
        'gm_v_g': 1.0 + nrm(ks[22], (N_ODD, D_GM), 0.02),
        'gm_v_b': nrm(ks[23], (N_ODD, D_GM), 0.02),
        'gm_w_sp': nrm(ks[24], (N_ODD, GM_GROUPS, CHUNK, CHUNK), CHUNK ** -0.5),
        'gm_b_sp': 1.0 + nrm(ks[25], (N_ODD, GM_GROUPS, CHUNK), 0.02),
        'gm_w_out': nrm(ks[26], (N_ODD, D_GM, D), D_GM ** -0.5),
    }


def reference(x, c, ctx, c_ctx, w_mod, b_mod, norm_g, w_ff_in, w_ff_out, ar_w_in, ar_q_g, ar_k_g, ar_conv_w, ar_conv_b, ar_wa, ar_ba, ar_wx, ar_bx, ar_lambda, ar_w_out, gm_w_in, gm_b_in, gm_v_g, gm_v_b, gm_w_sp, gm_b_sp, gm_w_out):
    xl, xc = x, ctx
    s_c = jax.nn.silu(c)
    s_ctx = jax.nn.silu(c_ctx)
    for i in range(DEPTH):
        j = i // 2
        need_ctx = any(l % 2 == 0 for l in range(i + 1, DEPTH))
        g = norm_g[i]
        ml = jnp.split((s_c @ w_mod[i] + b_mod[i])[:, None, :], 6, axis=-1)
        mc = jnp.split((s_ctx @ w_mod[i] + b_mod[i])[None, None, :], 6, axis=-1)
        hl = modulate(rms_norm(xl, g[0]), ml[0], ml[1])
        oc = None
        if i % 2 == 0:
            hc = modulate(rms_norm(xc, g[0]), mc[0], mc[1])
            ol, oc = mix_attn_rglru(hl, hc, ar_w_in[j], ar_q_g[j], ar_k_g[j], ar_conv_w[j], ar_conv_b[j],
                                    ar_wa[j], ar_ba[j], ar_wx[j], ar_bx[j], ar_lambda[j], ar_w_out[j], need_ctx)
        else:
            ol = chunk_gmlp(hl, gm_w_in[j], gm_b_in[j], gm_v_g[j], gm_v_b[j], gm_w_sp[j], gm_b_sp[j], gm_w_out[j])
            if need_ctx:
                hc = modulate(rms_norm(xc, g[0]), mc[0], mc[1])
                oc = chunk_gmlp(hc, gm_w_in[j], gm_b_in[j], gm_v_g[j], gm_v_b[j], gm_w_sp[j], gm_b_sp[j], gm_w_out[j])
        xl = xl + ml[2] * rms_norm(ol, g[1])
        hl = modulate(rms_norm(xl, g[2]), ml[3], ml[4])
        xl = xl + ml[5] * rms_norm(sq_relu_mlp(hl, w_ff_in[i], w_ff_out[i]), g[3])
        if need_ctx:
            xc = xc + mc[2] * rms_norm(oc, g[1])
            hc = modulate(rms_norm(xc, g[2]), mc[3], mc[4])
            xc = xc + mc[5] * rms_norm(sq_relu_mlp(hc, w_ff_in[i], w_ff_out[i]), g[3])
    return xl
```

```python
import functools

import jax
import jax.numpy as jnp
from jax import lax
from jax.experimental import pallas as pl
from jax.experimental.pallas import tpu as pltpu

F32 = jnp.float32
BF16 = jnp.bfloat16

GRID_W = 64
N_HEADS = 8
N_KV_HEADS = 2
HEAD_DIM = 128
ATTN_W = N_HEADS * HEAD_DIM
KV_W = N_KV_HEADS * HEAD_DIM
ROPE_THETA = 10000.0
ROPE_PAIRS = HEAD_DIM // 4
D_RNN = 1024
RNN_BLOCKS = 8
RNN_BLOCK_W = D_RNN // RNN_BLOCKS
CONV_W = 4
RG_C = 8.0
GM_GROUPS = 16
GM_GROUP_W = 128
CHUNK = 128
EPS = 1e-6

LANES = 128
SUBLANES = 8
VMEM_LIMIT_BYTES = 56 * 1024 * 1024

MOD_TN = 1024
PROJ_TN = 512
PROJ_TM = 1024
SCAN_TC = 256
ATTN_TQ = 256
OUT_TM = 512
FFN_TM = 512
FFN_TF = 512
GM_TM = 1024
GM_TN = 1024
GMO_TM = 256


def _params():
    return pltpu.CompilerParams(vmem_limit_bytes=VMEM_LIMIT_BYTES)


def _rms(x, g):
    ms = jnp.mean(x * x, axis=-1, keepdims=True)
    return x * lax.rsqrt(ms + EPS) * g


def _sigmoid(x):
    return 1.0 / (1.0 + jnp.exp(-x))


def _gelu(x):
    c = 0.7978845608028654
    return 0.5 * x * (1.0 + jnp.tanh(c * (x + 0.044715 * (x * x * x))))


def _mod_kernel(cc_ref, w_ref, b_ref, o_ref):
    c = cc_ref[...]
    s = c * _sigmoid(c)
    o_ref[0] = jnp.dot(s.astype(BF16), w_ref[0].astype(BF16), preferred_element_type=F32) + b_ref[0]


def _modulation(cc, w_mod, b_mod):
    depth, d, n = w_mod.shape
    rows = cc.shape[0]
    return pl.pallas_call(
        _mod_kernel,
        grid=(depth, n // MOD_TN),
        in_specs=[
            pl.BlockSpec((rows, d), lambda l, j: (0, 0)),
            pl.BlockSpec((1, d, MOD_TN), lambda l, j: (l, 0, j)),
            pl.BlockSpec((1, 1, MOD_TN), lambda l, j: (l, 0, j)),
        ],
        out_specs=pl.BlockSpec((1, rows, MOD_TN), lambda l, j: (l, 0, j)),
        out_shape=jax.ShapeDtypeStruct((depth, rows, n), F32),
        compiler_params=_params(),
        name="adaln_modulation",
    )(cc, w_mod, b_mod.reshape(depth, 1, n))


def _head_norm_rope(xh, gain, cos, sin):
    y = _rms(xh, gain)
    if cos is None:
        return y
    lane = lax.broadcasted_iota(jnp.int32, y.shape, 1)
    first_half = (lane % (2 * ROPE_PAIRS)) < ROPE_PAIRS
    partner = jnp.where(first_half,
                        pltpu.roll(y, HEAD_DIM - ROPE_PAIRS, 1),
                        pltpu.roll(y, ROPE_PAIRS, 1))
    return y * cos + partner * sin


def _in_proj_kernel(*refs, j0, latent):
    if latent:
        (x_ref, sh_ref, sc_ref, g_ref, w_ref, qg_ref, kg_ref, cos_ref, sin_ref,
         q_ref, k_ref, v_ref, xr_ref, gr_ref, h_scr) = refs
        cos, sin = cos_ref[...], sin_ref[...]
    else:
        (x_ref, sh_ref, sc_ref, g_ref, w_ref, qg_ref, kg_ref,
         k_ref, v_ref, xr_ref, h_scr) = refs
        cos = sin = None
    jj = pl.program_id(2) + j0

    @pl.when(pl.program_id(2) == 0)
    def _():
        h = _rms(x_ref[0], g_ref[...]) * (1.0 + sc_ref[0]) + sh_ref[0]
        h_scr[...] = h.astype(BF16)

    acc = jnp.dot(h_scr[...], w_ref[...], preferred_element_type=F32)
    heads_per_tile = PROJ_TN // HEAD_DIM
    q_tiles = ATTN_W // PROJ_TN

    if latent:
        @pl.when(jj < q_tiles)
        def _():
            for hh in range(heads_per_tile):
                sl = slice(hh * HEAD_DIM, (hh + 1) * HEAD_DIM)
                q_ref[0, :, sl] = _head_norm_rope(acc[:, sl], qg_ref[...], cos, sin).astype(BF16)

    @pl.when(jj == q_tiles)
    def _():
        for hh in range(N_KV_HEADS):
            sl = slice(hh * HEAD_DIM, (hh + 1) * HEAD_DIM)
            k_ref[0, :, sl] = _head_norm_rope(acc[:, sl], kg_ref[...], cos, sin).astype(BF16)
        v_ref[0] = acc[:, KV_W:2 * KV_W].astype(BF16)

    @pl.when(jnp.logical_and(jj > q_tiles, jj <= q_tiles + 2))
    def _():
        xr_ref[0] = acc

    if latent:
        @pl.when(jj > q_tiles + 2)
        def _():
            gr_ref[0] = acc


def _in_proj(x, shift, scale, g, w, qg, kg, cos, sin, *, latent):
    bn, length, d = x.shape
    tm = min(PROJ_TM, length)
    q_tiles = ATTN_W // PROJ_TN
    j0 = 0 if latent else q_tiles
    nj = (w.shape[1] // PROJ_TN) if latent else 3
    per_batch = shift.shape[0] > 1
    mod_map = (lambda b, i, j: (b, 0, 0)) if per_batch else (lambda b, i, j: (0, 0, 0))
    vec_map = lambda b, i, j: (0, 0)

    in_specs = [
        pl.BlockSpec((1, tm, d), lambda b, i, j: (b, i, 0)),
        pl.BlockSpec((1, 1, d), mod_map),
        pl.BlockSpec((1, 1, d), mod_map),
        pl.BlockSpec((1, d), vec_map),
        pl.BlockSpec((d, PROJ_TN), lambda b, i, j: (0, j + j0)),
        pl.BlockSpec((1, HEAD_DIM), vec_map),
        pl.BlockSpec((1, HEAD_DIM), vec_map),
    ]
    args = [x, shift, scale, g, w, qg, kg]
    k_spec = pl.BlockSpec((1, tm, KV_W), lambda b, i, j: (b, i, 0))
    xr_spec = pl.BlockSpec((1, tm, PROJ_TN),
                           lambda b, i, j: (b, i, jnp.clip(j + j0 - q_tiles - 1, 0, 1)))
    k_shape = jax.ShapeDtypeStruct((bn, length, KV_W), BF16)
    xr_shape = jax.ShapeDtypeStruct((bn, length, D_RNN), F32)
    if latent:
        in_specs += [pl.BlockSpec((tm, HEAD_DIM), lambda b, i, j: (i, 0))] * 2
        args += [cos, sin]
        out_specs = [
            pl.BlockSpec((1, tm, PROJ_TN), lambda b, i, j: (b, i, jnp.minimum(j, q_tiles - 1))),
            k_spec, k_spec, xr_spec,
            pl.BlockSpec((1, tm, PROJ_TN),
                         lambda b, i, j: (b, i, jnp.clip(j - q_tiles - 3, 0, 1))),
        ]
        out_shape = [jax.ShapeDtypeStruct((bn, length, ATTN_W), BF16), k_shape, k_shape, xr_shape, xr_shape]
    else:
        out_specs = [k_spec, k_spec, xr_spec]
        out_shape = [k_shape, k_shape, xr_shape]

    return pl.pallas_call(
        functools.partial(_in_proj_kernel, j0=j0, latent=latent),
        grid=(bn, length // tm, nj),
        in_specs=in_specs,
        out_specs=out_specs,
        out_shape=out_shape,
        scratch_shapes=[pltpu.VMEM((tm, d), BF16)],
        compiler_params=_params(),
        name="in_proj_latent" if latent else "in_proj_context",
    )(*args)


def _dwconv(x, w, b):
    n = x.shape[0]
    row = lax.broadcasted_iota(jnp.int32, x.shape, 0)
    y = b + x * w[2:3]
    y = y + jnp.where(row >= 2, pltpu.roll(x, 2, 0), 0.0) * w[0:1]
    y = y + jnp.where(row >= 1, pltpu.roll(x, 1, 0), 0.0) * w[1:2]
    y = y + jnp.where(row < n - 1, pltpu.roll(x, n - 1, 0), 0.0) * w[3:4]
    return y


def _scan_chunk(a, b, carry, reverse):
    tc = a.shape[0]
    nv = tc // SUBLANES
    a3 = a.reshape(nv, SUBLANES, LANES)
    b3 = b.reshape(nv, SUBLANES, LANES)
    row = lax.broadcasted_iota(jnp.int32, a3.shape, 1)
    step = 1
    while step < SUBLANES:
        if reverse:
            shift, valid = SUBLANES - step, row < SUBLANES - step
        else:
            shift, valid = step, row >= step
        a_sh = pltpu.roll(a3, shift, 1)
        b_sh = pltpu.roll(b3, shift, 1)
        b3 = jnp.where(valid, a3 * b_sh + b3, b3)
        a3 = jnp.where(valid, a3 * a_sh, a3)
        step *= 2
    last = 0 if reverse else SUBLANES - 1
    a_last = jnp.broadcast_to(a3[:, last:last + 1, :], a3.shape)
    b_last = jnp.broadcast_to(b3[:, last:last + 1, :], b3.shape)
    hs = [None] * nv
    for v in (range(nv - 1, -1, -1) if reverse else range(nv)):
        hs[v] = b3[v] + a3[v] * carry
        carry = b_last[v] + a_last[v] * carry
    return hs, carry


def _rglru_kernel(xl_ref, xc_ref, gl_ref, cw_ref, cb_ref, w_ref, bias_ref, lam_ref, o_ref,
                  xconv_l, xconv_c, hf_scr):
    cw = cw_ref[...]
    cb = cb_ref[...]
    xconv_l[...] = _dwconv(xl_ref[0], cw, cb)
    xconv_c[...] = _dwconv(xc_ref[0], cw, cb)
    n_lat = xl_ref.shape[1] // SCAN_TC
    n_ctx = xc_ref.shape[1] // SCAN_TC

    def gates(xc, d):
        z = jnp.dot(xc.astype(BF16), w_ref[d, 0], preferred_element_type=F32) + bias_ref[d, 0]
        r = _sigmoid(z[:, :RNN_BLOCK_W])
        i = _sigmoid(z[:, RNN_BLOCK_W:])
        lam = lam_ref[d, 0]
        softplus_neg_lam = jnp.maximum(-lam, 0.0) + jnp.log1p(jnp.exp(-jnp.abs(lam)))
        log_a = (-RG_C) * r * softplus_neg_lam
        a = jnp.exp(log_a)
        th = jnp.tanh(log_a)
        b = jnp.sqrt(-2.0 * th / (1.0 - th)) * (i * xc)
        return a, b

    for d, reverse in ((0, False), (1, True)):
        def ctx_step(kk, carry, d=d, reverse=reverse):
            k = (n_ctx - 1 - kk) if reverse else kk
            r0 = pl.multiple_of(k * SCAN_TC, SCAN_TC)
            a, b = gates(xconv_c[pl.ds(r0, SCAN_TC), :], d)
            _, carry = _scan_chunk(a, b, carry, reverse)
            return carry

        def lat_step(kk, carry, d=d, reverse=reverse):
            k = (n_lat - 1 - kk) if reverse else kk
            r0 = pl.multiple_of(k * SCAN_TC, SCAN_TC)
            a, b = gates(xconv_l[pl.ds(r0, SCAN_TC), :], d)
            hs, carry = _scan_chunk(a, b, carry, reverse)
            for v, h in enumerate(hs):
                rows = pl.ds(r0 + v * SUBLANES, SUBLANES)
                if d == 0:
                    hf_scr[rows, :] = h
                else:
                    hf_scr[rows, :] = hf_scr[rows, :] + h
            return carry

        carry = jnp.zeros((SUBLANES, LANES), F32)
        carry = lax.fori_loop(0, n_ctx, ctx_step, carry)
        lax.fori_loop(0, n_lat, lat_step, carry)

    def out_step(k, _):
        r0 = pl.multiple_of(k * SCAN_TC, SCAN_TC)
        rows = pl.ds(r0, SCAN_TC)
        o_ref[0, rows, :] = (hf_scr[rows, :] * _gelu(gl_ref[0, rows, :])).astype(BF16)
        return 0

    lax.fori_loop(0, n_lat, out_step, 0)


def _rglru(xl, xc, gl, conv_w, conv_b, w_gate, b_gate, lam):
    bn, s, _ = xl.shape
    ctx_len = xc.shape[1]
    bw = RNN_BLOCK_W
    col = lambda b, n: (b, 0, n)
    return pl.pallas_call(
        _rglru_kernel,
        grid=(bn, RNN_BLOCKS),
        in_specs=[
            pl.BlockSpec((1, s, bw), col),
            pl.BlockSpec((1, ctx_len, bw), col),
            pl.BlockSpec((1, s, bw), col),
            pl.BlockSpec((CONV_W, bw), lambda b, n: (0, n)),
            pl.BlockSpec((1, bw), lambda b, n: (0, n)),
            pl.BlockSpec((2, 1, bw, 2 * bw), lambda b, n: (0, n, 0, 0)),
            pl.BlockSpec((2, 1, 1, 2 * bw), lambda b, n: (0, n, 0, 0)),
            pl.BlockSpec((2, 1, 1, bw), lambda b, n: (0, n, 0, 0)),
        ],
        out_specs=pl.BlockSpec((1, s, bw), col),
        out_shape=jax.ShapeDtypeStruct((bn, s, D_RNN), BF16),
        scratch_shapes=[pltpu.VMEM((s, bw), F32), pltpu.VMEM((ctx_len, bw), F32), pltpu.VMEM((s, bw), F32)],
        compiler_params=_params(),
        name="rglru_bidirectional",
    )(xl, xc, gl, conv_w, conv_b, w_gate, b_gate, lam)


def _attn_kernel(q_ref, k_ref, v_ref, o_ref):
    k = k_ref[0]
    v = v_ref[0]
    scale = HEAD_DIM ** -0.5
    for hh in range(N_HEADS // N_KV_HEADS):
        sl = slice(hh * HEAD_DIM, (hh + 1) * HEAD_DIM)
        s = lax.dot_general(q_ref[0, :, sl], k, (((1,), (1,)), ((), ())),
                            preferred_element_type=F32) * scale
        m = jnp.max(s, axis=-1, keepdims=True)
        p = jnp.exp(s - m)
        l = jnp.sum(p, axis=-1, keepdims=True)
        o = jnp.dot(p.astype(BF16), v, preferred_element_type=F32)
        o_ref[0, :, sl] = (o / l).astype(BF16)


def _attention(q, k, v):
    bn, s, _ = q.shape
    t = k.shape[1]
    gw = ATTN_W // N_KV_HEADS
    return pl.pallas_call(
        _attn_kernel,
        grid=(bn, N_KV_HEADS, s // ATTN_TQ),
        in_specs=[
            pl.BlockSpec((1, ATTN_TQ, gw), lambda b, h, i: (b, i, h)),
            pl.BlockSpec((1, t, HEAD_DIM), lambda b, h, i: (b, 0, h)),
            pl.BlockSpec((1, t, HEAD_DIM), lambda b, h, i: (b, 0, h)),
        ],
        out_specs=pl.BlockSpec((1, ATTN_TQ, gw), lambda b, h, i: (b, i, h)),
        out_shape=jax.ShapeDtypeStruct((bn, s, ATTN_W), BF16),
        compiler_params=_params(),
        name="gqa_attention",
    )(q, k, v)


def _residual_and_next(x, mix, gate, g_post, g_next, shift_next, scale_next):
    x1 = x + gate * _rms(mix, g_post)
    h = _rms(x1, g_next) * (1.0 + scale_next) + shift_next
    return x1, h.astype(BF16)


def _out_proj_kernel(a1_ref, a2_ref, w_ref, x_ref, gate_ref, g1_ref, g2_ref, sh_ref, sc_ref, x1_ref, h_ref):
    ka = a1_ref.shape[2]
    mix = jnp.dot(a1_ref[0], w_ref[0:ka, :], preferred_element_type=F32)
    mix = mix + jnp.dot(a2_ref[0], w_ref[ka:, :], preferred_element_type=F32)
    x1, h = _residual_and_next(x_ref[0], mix, gate_ref[0], g1_ref[...], g2_ref[...], sh_ref[0], sc_ref[0])
    x1_ref[0] = x1
    h_ref[0] = h


def _out_proj(a1, a2, w, x, gate, g1, g2, shift, scale):
    bn, s, d = x.shape
    tm = OUT_TM
    row = lambda b, i: (b, i, 0)
    mod = lambda b, i: (b, 0, 0)
    vec = lambda b, i: (0, 0)
    return pl.pallas_call(
        _out_proj_kernel,
        grid=(bn, s // tm),
        in_specs=[
            pl.BlockSpec((1, tm, a1.shape[2]), row),
            pl.BlockSpec((1, tm, a2.shape[2]), row),
            pl.BlockSpec(w.shape, vec),
            pl.BlockSpec((1, tm, d), row),
            pl.BlockSpec((1, 1, d), mod),
            pl.BlockSpec((1, d), vec),
            pl.BlockSpec((1, d), vec),
            pl.BlockSpec((1, 1, d), mod),
            pl.BlockSpec((1, 1, d), mod),
        ],
        out_specs=[pl.BlockSpec((1, tm, d), row), pl.BlockSpec((1, tm, d), row)],
        out_shape=[jax.ShapeDtypeStruct((bn, s, d), F32), jax.ShapeDtypeStruct((bn, s, d), BF16)],
        compiler_params=_params(),
        name="out_proj_residual",
    )(a1, a2, w, x, gate, g1, g2, shift, scale)


def _ffn_kernel(*refs, has_next):
    if has_next:
        (h_ref, w1_ref, w2_ref, x_ref, gate_ref, g3_ref, gn_ref, sh_ref, sc_ref,
         x2_ref, hn_ref, acc) = refs
    else:
        h_ref, w1_ref, w2_ref, x_ref, gate_ref, g3_ref, x2_ref, acc = refs
    j = pl.program_id(2)
    t = jnp.dot(h_ref[0], w1_ref[...], preferred_element_type=F32)
    t = jnp.maximum(t, 0.0)
    t = (t * t).astype(BF16)
    part = jnp.dot(t, w2_ref[...], preferred_element_type=F32)

    @pl.when(j == 0)
    def _():
        acc[...] = part

    @pl.when(j > 0)
    def _():
        acc[...] += part

    @pl.when(j == pl.num_programs(2) - 1)
    def _():
        if has_next:
            x2, hn = _residual_and_next(x_ref[0], acc[...], gate_ref[0], g3_ref[...],
                                        gn_ref[...], sh_ref[0], sc_ref[0])
            x2_ref[0] = x2
            hn_ref[0] = hn
        else:
            x2_ref[0] = x_ref[0] + gate_ref[0] * _rms(acc[...], g3_ref[...])


def _ffn(h, w1, w2, x, gate, g3, nxt=None):
    bn, s, d = x.shape
    dff = w1.shape[1]
    tm, tf = FFN_TM, FFN_TF
    row = lambda b, i, j: (b, i, 0)
    mod = lambda b, i, j: (b, 0, 0)
    vec = lambda b, i, j: (0, 0)
    in_specs = [
        pl.BlockSpec((1, tm, d), row),
        pl.BlockSpec((d, tf), lambda b, i, j: (0, j)),
        pl.BlockSpec((tf, d), lambda b, i, j: (j, 0)),
        pl.BlockSpec((1, tm, d), row),
        pl.BlockSpec((1, 1, d), mod),
        pl.BlockSpec((1, d), vec),
    ]
    args = [h, w1, w2, x, gate, g3]
    out_specs = [pl.BlockSpec((1, tm, d), row)]
    out_shape = [jax.ShapeDtypeStruct((bn, s, d), F32)]
    if nxt is not None:
        in_specs += [pl.BlockSpec((1, d), vec), pl.BlockSpec((1, 1, d), mod), pl.BlockSpec((1, 1, d), mod)]
        args += list(nxt)
        out_specs.append(pl.BlockSpec((1, tm, d), row))
        out_shape.append(jax.ShapeDtypeStruct((bn, s, d), BF16))
    return pl.pallas_call(
        functools.partial(_ffn_kernel, has_next=nxt is not None),
        grid=(bn, s // tm, dff // tf),
        in_specs=in_specs,
        out_specs=out_specs,
        out_shape=out_shape,
        scratch_shapes=[pltpu.VMEM((tm, d), F32)],
        compiler_params=_params(),
        name="sq_relu_mlp",
    )(*args)


def _gm_in_kernel(h_ref, w_ref, b_ref, z_ref):
    z = jnp.dot(h_ref[0], w_ref[...], preferred_element_type=F32) + b_ref[...]
    z_ref[0] = _gelu(z)


def _gm_in(h, w, b):
    bn, s, d = h.shape
    n = w.shape[1]
    return pl.pallas_call(
        _gm_in_kernel,
        grid=(bn, s // GM_TM, n // GM_TN),
        in_specs=[
            pl.BlockSpec((1, GM_TM, d), lambda b_, i, j: (b_, i, 0)),
            pl.BlockSpec((d, GM_TN), lambda b_, i, j: (0, j)),
            pl.BlockSpec((1, GM_TN), lambda b_, i, j: (0, j)),
        ],
        out_specs=pl.BlockSpec((1, GM_TM, GM_TN), lambda b_, i, j: (b_, i, j)),
        out_shape=jax.ShapeDtypeStruct((bn, s, n), F32),
        compiler_params=_params(),
        name="gmlp_in_proj",
    )(h, w, b)


def _gm_out_kernel(u_ref, v_ref, vg_ref, vb_ref, wsp_ref, bsp_ref, w_ref, x_ref, gate_ref, g1_ref, g2_ref,
                   sh_ref, sc_ref, x1_ref, h_ref, vn_scr, gated_scr):
    v = v_ref[0]
    mu = jnp.mean(v, axis=-1, keepdims=True)
    vc = v - mu
    var = jnp.mean(vc * vc, axis=-1, keepdims=True)
    vn_scr[...] = (vc * lax.rsqrt(var + EPS) * vg_ref[...] + vb_ref[...]).astype(BF16)
    for c in range(v.shape[0] // CHUNK):
        rows = slice(c * CHUNK, (c + 1) * CHUNK)
        for g in range(GM_GROUPS):
            cols = slice(g * GM_GROUP_W, (g + 1) * GM_GROUP_W)
            sv = jnp.dot(wsp_ref[g], vn_scr[rows, cols], preferred_element_type=F32) + bsp_ref[:, g:g + 1]
            gated_scr[rows, cols] = (u_ref[0, rows, cols] * sv).astype(BF16)
    mix = jnp.dot(gated_scr[...], w_ref[...], preferred_element_type=F32)
    x1, h = _residual_and_next(x_ref[0], mix, gate_ref[0], g1_ref[...], g2_ref[...], sh_ref[0], sc_ref[0])
    x1_ref[0] = x1
    h_ref[0] = h


def _gm_out(z, vg, vb, wsp, bsp_t, w, x, gate, g1, g2, shift, scale):
    bn, s, d = x.shape
    dg = z.shape[2] // 2
    tm = GMO_TM
    row = lambda b, i: (b, i, 0)
    mod = lambda b, i: (b, 0, 0)
    vec = lambda b, i: (0, 0)
    return pl.pallas_call(
        _gm_out_kernel,
        grid=(bn, s // tm),
        in_specs=[
            pl.BlockSpec((1, tm, dg), lambda b, i: (b, i, 0)),
            pl.BlockSpec((1, tm, dg), lambda b, i: (b, i, 1)),
            pl.BlockSpec((1, dg), vec),
            pl.BlockSpec((1, dg), vec),
            pl.BlockSpec(wsp.shape, lambda b, i: (0, 0, 0)),
            pl.BlockSpec(bsp_t.shape, vec),
            pl.BlockSpec(w.shape, vec),
            pl.BlockSpec((1, tm, d), row),
            pl.BlockSpec((1, 1, d), mod),
            pl.BlockSpec((1, d), vec),
            pl.BlockSpec((1, d), vec),
            pl.BlockSpec((1, 1, d), mod),
            pl.BlockSpec((1, 1, d), mod),
        ],
        out_specs=[pl.BlockSpec((1, tm, d), row), pl.BlockSpec((1, tm, d), row)],
        out_shape=[jax.ShapeDtypeStruct((bn, s, d), F32), jax.ShapeDtypeStruct((bn, s, d), BF16)],
        scratch_shapes=[pltpu.VMEM((tm, dg), BF16), pltpu.VMEM((tm, dg), BF16)],
        compiler_params=_params(),
        name="gmlp_spatial_out_proj",
    )(z, z, vg, vb, wsp, bsp_t, w, x, gate, g1, g2, shift, scale)


def _rope_tables(n):
    t = jnp.arange(n)
    r_idx = (t // GRID_W).astype(F32)
    c_idx = (t % GRID_W).astype(F32)
    freqs = ROPE_THETA ** (-jnp.arange(ROPE_PAIRS, dtype=F32) / ROPE_PAIRS)
    ang_r = r_idx[:, None] * freqs
    ang_c = c_idx[:, None] * freqs
    cos = jnp.concatenate([jnp.cos(ang_r), jnp.cos(ang_r), jnp.cos(ang_c), jnp.cos(ang_c)], axis=-1)
    sin = jnp.concatenate([-jnp.sin(ang_r), jnp.sin(ang_r), -jnp.sin(ang_c), jnp.sin(ang_c)], axis=-1)
    return cos, sin


def kernel(x, c, ctx, c_ctx, w_mod, b_mod, norm_g, w_ff_in, w_ff_out, ar_w_in, ar_q_g, ar_k_g, ar_conv_w,
           ar_conv_b, ar_wa, ar_ba, ar_wx, ar_bx, ar_lambda, ar_w_out, gm_w_in, gm_b_in, gm_v_g, gm_v_b,
           gm_w_sp, gm_b_sp, gm_w_out):
    bn, s, d = x.shape
    depth = w_mod.shape[0]
    assert depth == 2, "layer pattern implemented for one attention/recurrent layer followed by one gMLP layer"

    rows = -(-(bn + 1) // SUBLANES) * SUBLANES
    cc = jnp.concatenate([c, c_ctx[None, :], jnp.zeros((rows - bn - 1, d), F32)], axis=0)
    mods = _modulation(cc, w_mod, b_mod)

    def lat_mod(layer, k):
        return mods[layer, :bn, k * d:(k + 1) * d].reshape(bn, 1, d)

    def ctx_mod(layer, k):
        return mods[layer, bn:bn + 1, k * d:(k + 1) * d].reshape(1, 1, d)

    g = norm_g.reshape(depth, 4, 1, d)

    w_in = ar_w_in[0].astype(BF16)
    qg = ar_q_g[0].reshape(1, HEAD_DIM)
    kg = ar_k_g[0].reshape(1, HEAD_DIM)
    cos, sin = _rope_tables(s)
    ql, kl, vl, xrl, grl = _in_proj(x, lat_mod(0, 0), lat_mod(0, 1), g[0, 0], w_in, qg, kg, cos, sin, latent=True)
    kc, vc, xrc = _in_proj(ctx, ctx_mod(0, 0), ctx_mod(0, 1), g[0, 0], w_in, qg, kg, None, None, latent=False)

    k_all = jnp.concatenate([kc, kl], axis=1)
    v_all = jnp.concatenate([vc, vl], axis=1)
    attn = _attention(ql, k_all, v_all)

    w_gate = jnp.concatenate([ar_wa[0], ar_wx[0]], axis=-1).astype(BF16)
    b_gate = jnp.concatenate([ar_ba[0].reshape(2, RNN_BLOCKS, 1, RNN_BLOCK_W),
                              ar_bx[0].reshape(2, RNN_BLOCKS, 1, RNN_BLOCK_W)], axis=-1)
    lam = ar_lambda[0].reshape(2, RNN_BLOCKS, 1, RNN_BLOCK_W)
    rnn = _rglru(xrl, xrc, grl, ar_conv_w[0], ar_conv_b[0].reshape(1, D_RNN), w_gate, b_gate, lam)

    x1, h = _out_proj(attn, rnn, ar_w_out[0].astype(BF16), x, lat_mod(0, 2), g[0, 1], g[0, 2],
                      lat_mod(0, 3), lat_mod(0, 4))
    x2, h = _ffn(h, w_ff_in[0].astype(BF16), w_ff_out[0].astype(BF16), x1, lat_mod(0, 5), g[0, 3],
                 nxt=(g[1, 0], lat_mod(1, 0), lat_mod(1, 1)))

    z = _gm_in(h, gm_w_in[0].astype(BF16), gm_b_in[0].reshape(1, -1))
    x3, h = _gm_out(z, gm_v_g[0].reshape(1, -1), gm_v_b[0].reshape(1, -1), gm_w_sp[0].astype(BF16),
                    gm_b_sp[0].T, gm_w_out[0].astype(BF16), x2, lat_mod(1, 2), g[1, 1], g[1, 2],
                    lat_mod(1, 3), lat_mod(1, 4))
    (x4,) = _ffn(h, w_ff_in[1].astype(BF16), w_ff_out[1].astype(BF16), x3, lat_mod(1, 5), g[1, 3])
    return x4
```

```python
import functools

import jax
import jax.numpy as jnp
from jax import lax
from jax.experimental import pallas as pl
from jax.experimental.pallas import tpu as pltpu

F32 = jnp.float32
BF16 = jnp.bfloat16

GRID_W = 64
N_HEADS = 8
N_KV_HEADS = 2
HEAD_DIM = 128
ATTN_W = N_HEADS * HEAD_DIM
KV_W = N_KV_HEADS * HEAD_DIM
ROPE_THETA = 10000.0
ROPE_PAIRS = HEAD_DIM // 4
D_RNN = 1024
RNN_BLOCKS = 8
RNN_BLOCK_W = D_RNN // RNN_BLOCKS
CONV_W = 4
RG_C = 8.0
GM_GROUPS = 16
GM_GROUP_W = 128
CHUNK = 128
EPS = 1e-6

LANES = 128
SUBLANES = 8
VMEM_LIMIT_BYTES = 56 * 1024 * 1024

MOD_TN = 1024
PROJ_TN = 512
PROJ_TM = 1024
SCAN_TC = 256
ATTN_TQ = 512
ATTN_SUB = 256
OUT_TM = 512
FFN_TM = 512
FFN_TA = 1024
FFN_TB = 256
GM_TM = 1024
GM_TN = 1024
GMO_TM = 256


def _params():
    return pltpu.CompilerParams(vmem_limit_bytes=VMEM_LIMIT_BYTES)


def _rms(x, g):
    ms = jnp.mean(x * x, axis=-1, keepdims=True)
    return x * lax.rsqrt(ms + EPS) * g


def _sigmoid(x):
    return 1.0 / (1.0 + jnp.exp(-x))


def _gelu(x):
    c = 0.7978845608028654
    return 0.5 * x * (1.0 + jnp.tanh(c * (x + 0.044715 * (x * x * x))))


def _mod_kernel(cc_ref, w_ref, b_ref, o_ref):
    c = cc_ref[...]
    s = c * _sigmoid(c)
    o_ref[0] = jnp.dot(s.astype(BF16), w_ref[0].astype(BF16), preferred_element_type=F32) + b_ref[0]


def _modulation(cc, w_mod, b_mod):
    depth, d, n = w_mod.shape
    rows = cc.shape[0]
    return pl.pallas_call(
        _mod_kernel,
        grid=(depth, n // MOD_TN),
        in_specs=[
            pl.BlockSpec((rows, d), lambda l, j: (0, 0)),
            pl.BlockSpec((1, d, MOD_TN), lambda l, j: (l, 0, j)),
            pl.BlockSpec((1, 1, MOD_TN), lambda l, j: (l, 0, j)),
        ],
        out_specs=pl.BlockSpec((1, rows, MOD_TN), lambda l, j: (l, 0, j)),
        out_shape=jax.ShapeDtypeStruct((depth, rows, n), F32),
        compiler_params=_params(),
        name="adaln_modulation",
    )(cc, w_mod, b_mod.reshape(depth, 1, n))


def _head_norm_rope(xh, gain, cos, sin):
    y = _rms(xh, gain)
    if cos is None:
        return y
    lane = lax.broadcasted_iota(jnp.int32, y.shape, 1)
    first_half = (lane % (2 * ROPE_PAIRS)) < ROPE_PAIRS
    partner = jnp.where(first_half,
                        pltpu.roll(y, HEAD_DIM - ROPE_PAIRS, 1),
                        pltpu.roll(y, ROPE_PAIRS, 1))
    return y * cos + partner * sin


def _in_proj_kernel(*refs, j0, latent):
    if latent:
        (x_ref, sh_ref, sc_ref, g_ref, w_ref, qg_ref, kg_ref, cos_ref, sin_ref,
         q_ref, k_ref, v_ref, xr_ref, gr_ref, h_scr) = refs
        cos, sin = cos_ref[...], sin_ref[...]
    else:
        (x_ref, sh_ref, sc_ref, g_ref, w_ref, qg_ref, kg_ref,
         k_ref, v_ref, xr_ref, h_scr) = refs
        cos = sin = None
    jj = pl.program_id(2) + j0

    @pl.when(pl.program_id(2) == 0)
    def _():
        h = _rms(x_ref[0], g_ref[...]) * (1.0 + sc_ref[0]) + sh_ref[0]
        h_scr[...] = h.astype(BF16)

    acc = jnp.dot(h_scr[...], w_ref[...], preferred_element_type=F32)
    heads_per_tile = PROJ_TN // HEAD_DIM
    q_tiles = ATTN_W // PROJ_TN

    if latent:
        @pl.when(jj < q_tiles)
        def _():
            for hh in range(heads_per_tile):
                sl = slice(hh * HEAD_DIM, (hh + 1) * HEAD_DIM)
                q_ref[0, :, sl] = _head_norm_rope(acc[:, sl], qg_ref[...], cos, sin).astype(BF16)

    @pl.when(jj == q_tiles)
    def _():
        for hh in range(N_KV_HEADS):
            sl = slice(hh * HEAD_DIM, (hh + 1) * HEAD_DIM)
            k_ref[0, :, sl] = _head_norm_rope(acc[:, sl], kg_ref[...], cos, sin).astype(BF16)
        v_ref[0] = acc[:, KV_W:2 * KV_W].astype(BF16)

    @pl.when(jnp.logical_and(jj > q_tiles, jj <= q_tiles + 2))
    def _():
        xr_ref[0] = acc

    if latent:
        @pl.when(jj > q_tiles + 2)
        def _():
            gr_ref[0] = acc


def _in_proj(x, shift, scale, g, w, qg, kg, cos, sin, *, latent):
    bn, length, d = x.shape
    tm = min(PROJ_TM, length)
    q_tiles = ATTN_W // PROJ_TN
    j0 = 0 if latent else q_tiles
    nj = (w.shape[1] // PROJ_TN) if latent else 3
    per_batch = shift.shape[0] > 1
    mod_map = (lambda b, i, j: (b, 0, 0)) if per_batch else (lambda b, i, j: (0, 0, 0))
    vec_map = lambda b, i, j: (0, 0)

    in_specs = [
        pl.BlockSpec((1, tm, d), lambda b, i, j: (b, i, 0)),
        pl.BlockSpec((1, 1, d), mod_map),
        pl.BlockSpec((1, 1, d), mod_map),
        pl.BlockSpec((1, d), vec_map),
        pl.BlockSpec((d, PROJ_TN), lambda b, i, j: (0, j + j0)),
        pl.BlockSpec((1, HEAD_DIM), vec_map),
        pl.BlockSpec((1, HEAD_DIM), vec_map),
    ]
    args = [x, shift, scale, g, w, qg, kg]
    k_spec = pl.BlockSpec((1, tm, KV_W), lambda b, i, j: (b, i, 0))
    xr_spec = pl.BlockSpec((1, tm, PROJ_TN),
                           lambda b, i, j: (b, i, jnp.clip(j + j0 - q_tiles - 1, 0, 1)))
    k_shape = jax.ShapeDtypeStruct((bn, length, KV_W), BF16)
    xr_shape = jax.ShapeDtypeStruct((bn, length, D_RNN), F32)
    if latent:
        in_specs += [pl.BlockSpec((tm, HEAD_DIM), lambda b, i, j: (i, 0))] * 2
        args += [cos, sin]
        out_specs = [
            pl.BlockSpec((1, tm, PROJ_TN), lambda b, i, j: (b, i, jnp.minimum(j, q_tiles - 1))),
            k_spec, k_spec, xr_spec,
            pl.BlockSpec((1, tm, PROJ_TN),
                         lambda b, i, j: (b, i, jnp.clip(j - q_tiles - 3, 0, 1))),
        ]
        out_shape = [jax.ShapeDtypeStruct((bn, length, ATTN_W), BF16), k_shape, k_shape, xr_shape, xr_shape]
    else:
        out_specs = [k_spec, k_spec, xr_spec]
        out_shape = [k_shape, k_shape, xr_shape]

    return pl.pallas_call(
        functools.partial(_in_proj_kernel, j0=j0, latent=latent),
        grid=(bn, length // tm, nj),
        in_specs=in_specs,
        out_specs=out_specs,
        out_shape=out_shape,
        scratch_shapes=[pltpu.VMEM((tm, d), BF16)],
        compiler_params=_params(),
        name="in_proj_latent" if latent else "in_proj_context",
    )(*args)


def _dwconv(x, w, b):
    n = x.shape[0]
    row = lax.broadcasted_iota(jnp.int32, x.shape, 0)
    y = b + x * w[2:3]
    y = y + jnp.where(row >= 2, pltpu.roll(x, 2, 0), 0.0) * w[0:1]
    y = y + jnp.where(row >= 1, pltpu.roll(x, 1, 0), 0.0) * w[1:2]
    y = y + jnp.where(row < n - 1, pltpu.roll(x, n - 1, 0), 0.0) * w[3:4]
    return y


def _scan_chunk(a, b, carry, reverse):
    tc = a.shape[0]
    nv = tc // SUBLANES
    a3 = a.reshape(nv, SUBLANES, LANES)
    b3 = b.reshape(nv, SUBLANES, LANES)
    row = lax.broadcasted_iota(jnp.int32, a3.shape, 1)
    step = 1
    while step < SUBLANES:
        if reverse:
            shift, valid = SUBLANES - step, row < SUBLANES - step
        else:
            shift, valid = step, row >= step
        a_sh = pltpu.roll(a3, shift, 1)
        b_sh = pltpu.roll(b3, shift, 1)
        b3 = jnp.where(valid, a3 * b_sh + b3, b3)
        a3 = jnp.where(valid, a3 * a_sh, a3)
        step *= 2
    last = 0 if reverse else SUBLANES - 1
    a_last = jnp.broadcast_to(a3[:, last:last + 1, :], a3.shape)
    b_last = jnp.broadcast_to(b3[:, last:last + 1, :], b3.shape)
    hs = [None] * nv
    for v in (range(nv - 1, -1, -1) if reverse else range(nv)):
        hs[v] = b3[v] + a3[v] * carry
        carry = b_last[v] + a_last[v] * carry
    return hs, carry


def _rglru_kernel(xl_ref, xc_ref, gl_ref, cw_ref, cb_ref, w_ref, bias_ref, lam_ref, o_ref,
                  xconv_l, xconv_c, hf_scr):
    cw = cw_ref[...]
    cb = cb_ref[...]
    xconv_l[...] = _dwconv(xl_ref[0], cw, cb)
    xconv_c[...] = _dwconv(xc_ref[0], cw, cb)
    n_lat = xl_ref.shape[1] // SCAN_TC
    n_ctx = xc_ref.shape[1] // SCAN_TC

    def gates(xc, d):
        z = jnp.dot(xc.astype(BF16), w_ref[d, 0], preferred_element_type=F32) + bias_ref[d, 0]
        t_r = jnp.tanh(0.5 * z[:, :RNN_BLOCK_W])
        i = 0.5 * jnp.tanh(0.5 * z[:, RNN_BLOCK_W:]) + 0.5
        lam = lam_ref[d, 0]
        softplus_neg_lam = jnp.maximum(-lam, 0.0) + jnp.log1p(jnp.exp(-jnp.abs(lam)))
        half_rate = (-0.5 * RG_C) * softplus_neg_lam
        log_a = half_rate * t_r + half_rate
        a = jnp.exp(log_a)
        th = jnp.tanh(log_a)
        b = jnp.sqrt(-2.0 * th / (1.0 - th)) * (i * xc)
        return a, b

    for d, reverse in ((0, False), (1, True)):
        def ctx_step(kk, carry, d=d, reverse=reverse):
            k = (n_ctx - 1 - kk) if reverse else kk
            r0 = pl.multiple_of(k * SCAN_TC, SCAN_TC)
            a, b = gates(xconv_c[pl.ds(r0, SCAN_TC), :], d)
            _, carry = _scan_chunk(a, b, carry, reverse)
            return carry

        def lat_step(kk, carry, d=d, reverse=reverse):
            k = (n_lat - 1 - kk) if reverse else kk
            r0 = pl.multiple_of(k * SCAN_TC, SCAN_TC)
            a, b = gates(xconv_l[pl.ds(r0, SCAN_TC), :], d)
            hs, carry = _scan_chunk(a, b, carry, reverse)
            for v, h in enumerate(hs):
                rows = pl.ds(r0 + v * SUBLANES, SUBLANES)
                if d == 0:
                    hf_scr[rows, :] = h
                else:
                    hf_scr[rows, :] = hf_scr[rows, :] + h
            return carry

        carry = jnp.zeros((SUBLANES, LANES), F32)
        carry = lax.fori_loop(0, n_ctx, ctx_step, carry)
        lax.fori_loop(0, n_lat, lat_step, carry)

    def out_step(k, _):
        r0 = pl.multiple_of(k * SCAN_TC, SCAN_TC)
        rows = pl.ds(r0, SCAN_TC)
        o_ref[0, rows, :] = (hf_scr[rows, :] * _gelu(gl_ref[0, rows, :])).astype(BF16)
        return 0

    lax.fori_loop(0, n_lat, out_step, 0)


def _rglru(xl, xc, gl, conv_w, conv_b, w_gate, b_gate, lam):
    bn, s, _ = xl.shape
    ctx_len = xc.shape[1]
    bw = RNN_BLOCK_W
    col = lambda b, n: (b, 0, n)
    return pl.pallas_call(
        _rglru_kernel,
        grid=(bn, RNN_BLOCKS),
        in_specs=[
            pl.BlockSpec((1, s, bw), col),
            pl.BlockSpec((1, ctx_len, bw), col),
            pl.BlockSpec((1, s, bw), col),
            pl.BlockSpec((CONV_W, bw), lambda b, n: (0, n)),
            pl.BlockSpec((1, bw), lambda b, n: (0, n)),
            pl.BlockSpec((2, 1, bw, 2 * bw), lambda b, n: (0, n, 0, 0)),
            pl.BlockSpec((2, 1, 1, 2 * bw), lambda b, n: (0, n, 0, 0)),
            pl.BlockSpec((2, 1, 1, bw), lambda b, n: (0, n, 0, 0)),
        ],
        out_specs=pl.BlockSpec((1, s, bw), col),
        out_shape=jax.ShapeDtypeStruct((bn, s, D_RNN), BF16),
        scratch_shapes=[pltpu.VMEM((s, bw), F32), pltpu.VMEM((ctx_len, bw), F32), pltpu.VMEM((s, bw), F32)],
        compiler_params=_params(),
        name="rglru_bidirectional",
    )(xl, xc, gl, conv_w, conv_b, w_gate, b_gate, lam)


def _attn_kernel(q_ref, k_ref, v_ref, o_ref):
    k = k_ref[0]
    v = v_ref[0]
    c = (HEAD_DIM ** -0.5) * 1.4426950408889634
    for r0 in range(0, q_ref.shape[1], ATTN_SUB):
        rows = slice(r0, r0 + ATTN_SUB)
        for hh in range(N_HEADS // N_KV_HEADS):
            sl = slice(hh * HEAD_DIM, (hh + 1) * HEAD_DIM)
            s = lax.dot_general(q_ref[0, rows, sl], k, (((1,), (1,)), ((), ())), preferred_element_type=F32)
            m = jnp.max(s, axis=-1, keepdims=True)
            p = jnp.exp2((s - m) * c).astype(BF16)
            ov = jnp.dot(p, v, preferred_element_type=F32)
            o_ref[0, rows, sl] = (ov[:, :HEAD_DIM] / ov[:, HEAD_DIM:HEAD_DIM + 1]).astype(BF16)


def _attention(q, k, v_ones):
    bn, s, _ = q.shape
    t = k.shape[1]
    gw = ATTN_W // N_KV_HEADS
    return pl.pallas_call(
        _attn_kernel,
        grid=(bn, N_KV_HEADS, s // ATTN_TQ),
        in_specs=[
            pl.BlockSpec((1, ATTN_TQ, gw), lambda b, h, i: (b, i, h)),
            pl.BlockSpec((1, t, HEAD_DIM), lambda b, h, i: (b, 0, h)),
            pl.BlockSpec((1, t, 2 * HEAD_DIM), lambda b, h, i: (b, 0, h)),
        ],
        out_specs=pl.BlockSpec((1, ATTN_TQ, gw), lambda b, h, i: (b, i, h)),
        out_shape=jax.ShapeDtypeStruct((bn, s, ATTN_W), BF16),
        compiler_params=_params(),
        name="gqa_attention",
    )(q, k, v_ones)


def _residual_and_next(x, mix, gate, g_post, g_next, shift_next, scale_next):
    x1 = x + gate * _rms(mix, g_post)
    h = _rms(x1, g_next) * (1.0 + scale_next) + shift_next
    return x1, h.astype(BF16)


def _out_proj_kernel(a1_ref, a2_ref, w_ref, x_ref, gate_ref, g1_ref, g2_ref, sh_ref, sc_ref, x1_ref, h_ref):
    ka = a1_ref.shape[2]
    mix = jnp.dot(a1_ref[0], w_ref[0:ka, :], preferred_element_type=F32)
    mix = mix + jnp.dot(a2_ref[0], w_ref[ka:, :], preferred_element_type=F32)
    x1, h = _residual_and_next(x_ref[0], mix, gate_ref[0], g1_ref[...], g2_ref[...], sh_ref[0], sc_ref[0])
    x1_ref[0] = x1
    h_ref[0] = h


def _out_proj(a1, a2, w, x, gate, g1, g2, shift, scale):
    bn, s, d = x.shape
    tm = OUT_TM
    row = lambda b, i: (b, i, 0)
    mod = lambda b, i: (b, 0, 0)
    vec = lambda b, i: (0, 0)
    return pl.pallas_call(
        _out_proj_kernel,
        grid=(bn, s // tm),
        in_specs=[
            pl.BlockSpec((1, tm, a1.shape[2]), row),
            pl.BlockSpec((1, tm, a2.shape[2]), row),
            pl.BlockSpec(w.shape, vec),
            pl.BlockSpec((1, tm, d), row),
            pl.BlockSpec((1, 1, d), mod),
            pl.BlockSpec((1, d), vec),
            pl.BlockSpec((1, d), vec),
            pl.BlockSpec((1, 1, d), mod),
            pl.BlockSpec((1, 1, d), mod),
        ],
        out_specs=[pl.BlockSpec((1, tm, d), row), pl.BlockSpec((1, tm, d), row)],
        out_shape=[jax.ShapeDtypeStruct((bn, s, d), F32), jax.ShapeDtypeStruct((bn, s, d), BF16)],
        compiler_params=_params(),
        name="out_proj_residual",
    )(a1, a2, w, x, gate, g1, g2, shift, scale)


def _ffn_kernel(*refs, has_next, na):
    if has_next:
        (h_ref, w1_ref, w2_ref, x_ref, gate_ref, g3_ref, gn_ref, sh_ref, sc_ref,
         x2_ref, hn_ref, t_scr) = refs
    else:
        h_ref, w1_ref, w2_ref, x_ref, gate_ref, g3_ref, x2_ref, t_scr = refs
    j = pl.program_id(2)
    ta = t_scr.shape[2]
    tb = w2_ref.shape[1]

    @pl.when(j < na)
    def _():
        t = jnp.dot(h_ref[0], w1_ref[...], preferred_element_type=F32)
        t = jnp.maximum(t, 0.0)
        t_scr[j] = (t * t).astype(BF16)

    @pl.when(j >= na)
    def _():
        y = jnp.dot(t_scr[0], w2_ref[0:ta, :], preferred_element_type=F32)
        for a in range(1, na):
            y = y + jnp.dot(t_scr[a], w2_ref[a * ta:(a + 1) * ta, :], preferred_element_type=F32)
        col = pl.multiple_of((j - na) * tb, tb)
        x2_ref[0, :, pl.ds(col, tb)] = y

    @pl.when(j == pl.num_programs(2) - 1)
    def _():
        if has_next:
            x2, hn = _residual_and_next(x_ref[0], x2_ref[0], gate_ref[0], g3_ref[...],
                                        gn_ref[...], sh_ref[0], sc_ref[0])
            x2_ref[0] = x2
            hn_ref[0] = hn
        else:
            x2_ref[0] = x_ref[0] + gate_ref[0] * _rms(x2_ref[0], g3_ref[...])


def _ffn(h, w1, w2, x, gate, g3, nxt=None):
    bn, s, d = x.shape
    dff = w1.shape[1]
    tm, ta, tb = FFN_TM, FFN_TA, FFN_TB
    na, nb = dff // ta, d // tb
    row = lambda b, i, j: (b, i, 0)
    mod = lambda b, i, j: (b, 0, 0)
    vec = lambda b, i, j: (0, 0)
    in_specs = [
        pl.BlockSpec((1, tm, d), row),
        pl.BlockSpec((d, ta), lambda b, i, j: (0, jnp.minimum(j, na - 1))),
        pl.BlockSpec((dff, tb), lambda b, i, j: (0, jnp.maximum(j - na, 0))),
        pl.BlockSpec((1, tm, d), row),
        pl.BlockSpec((1, 1, d), mod),
        pl.BlockSpec((1, d), vec),
    ]
    args = [h, w1, w2, x, gate, g3]
    out_specs = [pl.BlockSpec((1, tm, d), row)]
    out_shape = [jax.ShapeDtypeStruct((bn, s, d), F32)]
    if nxt is not None:
        in_specs += [pl.BlockSpec((1, d), vec), pl.BlockSpec((1, 1, d), mod), pl.BlockSpec((1, 1, d), mod)]
        args += list(nxt)
        out_specs.append(pl.BlockSpec((1, tm, d), row))
        out_shape.append(jax.ShapeDtypeStruct((bn, s, d), BF16))
    return pl.pallas_call(
        functools.partial(_ffn_kernel, has_next=nxt is not None, na=na),
        grid=(bn, s // tm, na + nb),
        in_specs=in_specs,
        out_specs=out_specs,
        out_shape=out_shape,
        scratch_shapes=[pltpu.VMEM((na, tm, ta), BF16)],
        compiler_params=_params(),
        name="sq_relu_mlp",
    )(*args)


def _gm_in_kernel(h_ref, w_ref, b_ref, z_ref):
    z = jnp.dot(h_ref[0], w_ref[...], preferred_element_type=F32) + b_ref[...]
    z_ref[0] = _gelu(z)


def _gm_in(h, w, b):
    bn, s, d = h.shape
    n = w.shape[1]
    return pl.pallas_call(
        _gm_in_kernel,
        grid=(bn, s // GM_TM, n // GM_TN),
        in_specs=[
            pl.BlockSpec((1, GM_TM, d), lambda b_, i, j: (b_, i, 0)),
            pl.BlockSpec((d, GM_TN), lambda b_, i, j: (0, j)),
            pl.BlockSpec((1, GM_TN), lambda b_, i, j: (0, j)),
        ],
        out_specs=pl.BlockSpec((1, GM_TM, GM_TN), lambda b_, i, j: (b_, i, j)),
        out_shape=jax.ShapeDtypeStruct((bn, s, n), F32),
        compiler_params=_params(),
        name="gmlp_in_proj",
    )(h, w, b)


def _gm_out_kernel(u_ref, v_ref, vg_ref, vb_ref, wsp_ref, bsp_ref, w_ref, x_ref, gate_ref, g1_ref, g2_ref,
                   sh_ref, sc_ref, x1_ref, h_ref, vn_scr, gated_scr):
    v = v_ref[0]
    mu = jnp.mean(v, axis=-1, keepdims=True)
    vc = v - mu
    var = jnp.mean(vc * vc, axis=-1, keepdims=True)
    vn_scr[...] = (vc * lax.rsqrt(var + EPS) * vg_ref[...] + vb_ref[...]).astype(BF16)
    for c in range(v.shape[0] // CHUNK):
        rows = slice(c * CHUNK, (c + 1) * CHUNK)
        for g in range(GM_GROUPS):
            cols = slice(g * GM_GROUP_W, (g + 1) * GM_GROUP_W)
            sv = jnp.dot(wsp_ref[g], vn_scr[rows, cols], preferred_element_type=F32) + bsp_ref[:, g:g + 1]
            gated_scr[rows, cols] = (u_ref[0, rows, cols] * sv).astype(BF16)
    mix = jnp.dot(gated_scr[...], w_ref[...], preferred_element_type=F32)
    x1, h = _residual_and_next(x_ref[0], mix, gate_ref[0], g1_ref[...], g2_ref[...], sh_ref[0], sc_ref[0])
    x1_ref[0] = x1
    h_ref[0] = h


def _gm_out(z, vg, vb, wsp, bsp_t, w, x, gate, g1, g2, shift, scale):
    bn, s, d = x.shape
    dg = z.shape[2] // 2
    tm = GMO_TM
    row = lambda b, i: (b, i, 0)
    mod = lambda b, i: (b, 0, 0)
    vec = lambda b, i: (0, 0)
    return pl.pallas_call(
        _gm_out_kernel,
        grid=(bn, s // tm),
        in_specs=[
            pl.BlockSpec((1, tm, dg), lambda b, i: (b, i, 0)),
            pl.BlockSpec((1, tm, dg), lambda b, i: (b, i, 1)),
            pl.BlockSpec((1, dg), vec),
            pl.BlockSpec((1, dg), vec),
            pl.BlockSpec(wsp.shape, lambda b, i: (0, 0, 0)),
            pl.BlockSpec(bsp_t.shape, vec),
            pl.BlockSpec(w.shape, vec),
            pl.BlockSpec((1, tm, d), row),
            pl.BlockSpec((1, 1, d), mod),
            pl.BlockSpec((1, d), vec),
            pl.BlockSpec((1, d), vec),
            pl.BlockSpec((1, 1, d), mod),
            pl.BlockSpec((1, 1, d), mod),
        ],
        out_specs=[pl.BlockSpec((1, tm, d), row), pl.BlockSpec((1, tm, d), row)],
        out_shape=[jax.ShapeDtypeStruct((bn, s, d), F32), jax.ShapeDtypeStruct((bn, s, d), BF16)],
        scratch_shapes=[pltpu.VMEM((tm, dg), BF16), pltpu.VMEM((tm, dg), BF16)],
        compiler_params=_params(),
        name="gmlp_spatial_out_proj",
    )(z, z, vg, vb, wsp, bsp_t, w, x, gate, g1, g2, shift, scale)


def _rope_tables(n):
    t = jnp.arange(n)
    r_idx = (t // GRID_W).astype(F32)
    c_idx = (t % GRID_W).astype(F32)
    freqs = ROPE_THETA ** (-jnp.arange(ROPE_PAIRS, dtype=F32) / ROPE_PAIRS)
    ang_r = r_idx[:, None] * freqs
    ang_c = c_idx[:, None] * freqs
    cos = jnp.concatenate([jnp.cos(ang_r), jnp.cos(ang_r), jnp.cos(ang_c), jnp.cos(ang_c)], axis=-1)
    sin = jnp.concatenate([-jnp.sin(ang_r), jnp.sin(ang_r), -jnp.sin(ang_c), jnp.sin(ang_c)], axis=-1)
    return cos, sin


def kernel(x, c, ctx, c_ctx, w_mod, b_mod, norm_g, w_ff_in, w_ff_out, ar_w_in, ar_q_g, ar_k_g, ar_conv_w,
           ar_conv_b, ar_wa, ar_ba, ar_wx, ar_bx, ar_lambda, ar_w_out, gm_w_in, gm_b_in, gm_v_g, gm_v_b,
           gm_w_sp, gm_b_sp, gm_w_out):
    bn, s, d = x.shape
    depth = w_mod.shape[0]
    assert depth == 2, "layer pattern implemented for one attention/recurrent layer followed by one gMLP layer"

    rows = -(-(bn + 1) // SUBLANES) * SUBLANES
    cc = jnp.concatenate([c, c_ctx[None, :], jnp.zeros((rows - bn - 1, d), F32)], axis=0)
    mods = _modulation(cc, w_mod, b_mod)

    def lat_mod(layer, k):
        return mods[layer, :bn, k * d:(k + 1) * d].reshape(bn, 1, d)

    def ctx_mod(layer, k):
        return mods[layer, bn:bn + 1, k * d:(k + 1) * d].reshape(1, 1, d)

    g = norm_g.reshape(depth, 4, 1, d)

    w_in = ar_w_in[0].astype(BF16)
    qg = ar_q_g[0].reshape(1, HEAD_DIM)
    kg = ar_k_g[0].reshape(1, HEAD_DIM)
    cos, sin = _rope_tables(s)
    ql, kl, vl, xrl, grl = _in_proj(x, lat_mod(0, 0), lat_mod(0, 1), g[0, 0], w_in, qg, kg, cos, sin, latent=True)
    kc, vc, xrc = _in_proj(ctx, ctx_mod(0, 0), ctx_mod(0, 1), g[0, 0], w_in, qg, kg, None, None, latent=False)

    k_all = jnp.concatenate([kc, kl], axis=1)
    v_all = jnp.concatenate([vc, vl], axis=1).reshape(bn, -1, N_KV_HEADS, HEAD_DIM)
    v_ones = jnp.concatenate([v_all, jnp.ones_like(v_all)], axis=-1).reshape(bn, -1, 2 * KV_W)
    attn = _attention(ql, k_all, v_ones)

    w_gate = jnp.concatenate([ar_wa[0], ar_wx[0]], axis=-1).astype(BF16)
    b_gate = jnp.concatenate([ar_ba[0].reshape(2, RNN_BLOCKS, 1, RNN_BLOCK_W),
                              ar_bx[0].reshape(2, RNN_BLOCKS, 1, RNN_BLOCK_W)], axis=-1)
    lam = ar_lambda[0].reshape(2, RNN_BLOCKS, 1, RNN_BLOCK_W)
    rnn = _rglru(xrl, xrc, grl, ar_conv_w[0], ar_conv_b[0].reshape(1, D_RNN), w_gate, b_gate, lam)

    x1, h = _out_proj(attn, rnn, ar_w_out[0].astype(BF16), x, lat_mod(0, 2), g[0, 1], g[0, 2],
                      lat_mod(0, 3), lat_mod(0, 4))
    x2, h = _ffn(h, w_ff_in[0].astype(BF16), w_ff_out[0].astype(BF16), x1, lat_mod(0, 5), g[0, 3],
                 nxt=(g[1, 0], lat_mod(1, 0), lat_mod(1, 1)))

    z = _gm_in(h, gm_w_in[0].astype(BF16), gm_b_in[0].reshape(1, -1))
    x3, h = _gm_out(z, gm_v_g[0].reshape(1, -1), gm_v_b[0].reshape(1, -1), gm_w_sp[0].astype(BF16),
                    gm_b_sp[0].T, gm_w_out[0].astype(BF16), x2, lat_mod(1, 2), g[1, 1], g[1, 2],
                    lat_mod(1, 3), lat_mod(1, 4))
    (x4,) = _ffn(h, w_ff_in[1].astype(BF16), w_ff_out[1].astype(BF16), x3, lat_mod(1, 5), g[1, 3])
    return x4
```

```python
import functools

import jax
import jax.numpy as jnp
from jax import lax
from jax.experimental import pallas as pl
from jax.experimental.pallas import tpu as pltpu

F32 = jnp.float32
BF16 = jnp.bfloat16

GRID_W = 64
N_HEADS = 8
N_KV_HEADS = 2
HEAD_DIM = 128
ATTN_W = N_HEADS * HEAD_DIM
KV_W = N_KV_HEADS * HEAD_DIM
ROPE_THETA = 10000.0
ROPE_PAIRS = HEAD_DIM // 4
D_RNN = 1024
RNN_BLOCKS = 8
RNN_BLOCK_W = D_RNN // RNN_BLOCKS
CONV_W = 4
RG_C = 8.0
GM_GROUPS = 16
GM_GROUP_W = 128
CHUNK = 128
EPS = 1e-6

LANES = 128
SUBLANES = 8
VMEM_LIMIT_BYTES = 56 * 1024 * 1024

MOD_TN = 1024
PROJ_TN = 512
PROJ_TM = 1024
SCAN_TC = 256
SCAN_TJ = 32
SEG_PAD = 8
ATTN_TQ = 512
ATTN_SUB = 256
OUT_TM = 512
FFN_TM = 512
FFN_TA = 1024
FFN_TB = 256
GM_TM = 1024
GM_TN = 1024
GMO_TM = 256


def _params():
    return pltpu.CompilerParams(vmem_limit_bytes=VMEM_LIMIT_BYTES)


def _rms(x, g):
    ms = jnp.mean(x * x, axis=-1, keepdims=True)
    return x * lax.rsqrt(ms + EPS) * g


def _sigmoid(x):
    return 1.0 / (1.0 + jnp.exp(-x))


def _gelu(x):
    c = 0.7978845608028654
    return 0.5 * x * (1.0 + jnp.tanh(c * (x + 0.044715 * (x * x * x))))


def _mod_kernel(cc_ref, w_ref, b_ref, o_ref):
    c = cc_ref[...]
    s = c * _sigmoid(c)
    o_ref[0] = jnp.dot(s.astype(BF16), w_ref[0].astype(BF16), preferred_element_type=F32) + b_ref[0]


def _modulation(cc, w_mod, b_mod):
    depth, d, n = w_mod.shape
    rows = cc.shape[0]
    return pl.pallas_call(
        _mod_kernel,
        grid=(depth, n // MOD_TN),
        in_specs=[
            pl.BlockSpec((rows, d), lambda l, j: (0, 0)),
            pl.BlockSpec((1, d, MOD_TN), lambda l, j: (l, 0, j)),
            pl.BlockSpec((1, 1, MOD_TN), lambda l, j: (l, 0, j)),
        ],
        out_specs=pl.BlockSpec((1, rows, MOD_TN), lambda l, j: (l, 0, j)),
        out_shape=jax.ShapeDtypeStruct((depth, rows, n), F32),
        compiler_params=_params(),
        name="adaln_modulation",
    )(cc, w_mod, b_mod.reshape(depth, 1, n))


def _head_norm_rope(xh, gain, cos, sin):
    y = _rms(xh, gain)
    if cos is None:
        return y
    lane = lax.broadcasted_iota(jnp.int32, y.shape, 1)
    first_half = (lane % (2 * ROPE_PAIRS)) < ROPE_PAIRS
    partner = jnp.where(first_half,
                        pltpu.roll(y, HEAD_DIM - ROPE_PAIRS, 1),
                        pltpu.roll(y, ROPE_PAIRS, 1))
    return y * cos + partner * sin


def _in_proj_kernel(*refs, j0, latent):
    if latent:
        (x_ref, sh_ref, sc_ref, g_ref, w_ref, qg_ref, kg_ref, cos_ref, sin_ref,
         q_ref, k_ref, v_ref, xr_ref, gr_ref, h_scr) = refs
        cos, sin = cos_ref[...], sin_ref[...]
    else:
        (x_ref, sh_ref, sc_ref, g_ref, w_ref, qg_ref, kg_ref,
         k_ref, v_ref, xr_ref, h_scr) = refs
        cos = sin = None
    jj = pl.program_id(2) + j0

    @pl.when(pl.program_id(2) == 0)
    def _():
        h = _rms(x_ref[0], g_ref[...]) * (1.0 + sc_ref[0]) + sh_ref[0]
        h_scr[...] = h.astype(BF16)

    acc = jnp.dot(h_scr[...], w_ref[...], preferred_element_type=F32)
    heads_per_tile = PROJ_TN // HEAD_DIM
    q_tiles = ATTN_W // PROJ_TN

    if latent:
        @pl.when(jj < q_tiles)
        def _():
            for hh in range(heads_per_tile):
                sl = slice(hh * HEAD_DIM, (hh + 1) * HEAD_DIM)
                q_ref[0, :, sl] = _head_norm_rope(acc[:, sl], qg_ref[...], cos, sin).astype(BF16)

    @pl.when(jj == q_tiles)
    def _():
        for hh in range(N_KV_HEADS):
            sl = slice(hh * HEAD_DIM, (hh + 1) * HEAD_DIM)
            k_ref[0, :, sl] = _head_norm_rope(acc[:, sl], kg_ref[...], cos, sin).astype(BF16)
        v_ref[0] = acc[:, KV_W:2 * KV_W].astype(BF16)

    @pl.when(jnp.logical_and(jj > q_tiles, jj <= q_tiles + 2))
    def _():
        xr_ref[0] = acc

    if latent:
        @pl.when(jj > q_tiles + 2)
        def _():
            gr_ref[0] = acc


def _in_proj(x, shift, scale, g, w, qg, kg, cos, sin, *, latent):
    bn, length, d = x.shape
    tm = min(PROJ_TM, length)
    q_tiles = ATTN_W // PROJ_TN
    j0 = 0 if latent else q_tiles
    nj = (w.shape[1] // PROJ_TN) if latent else 3
    per_batch = shift.shape[0] > 1
    mod_map = (lambda b, i, j: (b, 0, 0)) if per_batch else (lambda b, i, j: (0, 0, 0))
    vec_map = lambda b, i, j: (0, 0)

    in_specs = [
        pl.BlockSpec((1, tm, d), lambda b, i, j: (b, i, 0)),
        pl.BlockSpec((1, 1, d), mod_map),
        pl.BlockSpec((1, 1, d), mod_map),
        pl.BlockSpec((1, d), vec_map),
        pl.BlockSpec((d, PROJ_TN), lambda b, i, j: (0, j + j0)),
        pl.BlockSpec((1, HEAD_DIM), vec_map),
        pl.BlockSpec((1, HEAD_DIM), vec_map),
    ]
    args = [x, shift, scale, g, w, qg, kg]
    k_spec = pl.BlockSpec((1, tm, KV_W), lambda b, i, j: (b, i, 0))
    xr_spec = pl.BlockSpec((1, tm, PROJ_TN),
                           lambda b, i, j: (b, i, jnp.clip(j + j0 - q_tiles - 1, 0, 1)))
    k_shape = jax.ShapeDtypeStruct((bn, length, KV_W), BF16)
    xr_shape = jax.ShapeDtypeStruct((bn, length, D_RNN), F32)
    if latent:
        in_specs += [pl.BlockSpec((tm, HEAD_DIM), lambda b, i, j: (i, 0))] * 2
        args += [cos, sin]
        out_specs = [
            pl.BlockSpec((1, tm, PROJ_TN), lambda b, i, j: (b, i, jnp.minimum(j, q_tiles - 1))),
            k_spec, k_spec, xr_spec,
            pl.BlockSpec((1, tm, PROJ_TN),
                         lambda b, i, j: (b, i, jnp.clip(j - q_tiles - 3, 0, 1))),
        ]
        out_shape = [jax.ShapeDtypeStruct((bn, length, ATTN_W), BF16), k_shape, k_shape, xr_shape, xr_shape]
    else:
        out_specs = [k_spec, k_spec, xr_spec]
        out_shape = [k_shape, k_shape, xr_shape]

    return pl.pallas_call(
        functools.partial(_in_proj_kernel, j0=j0, latent=latent),
        grid=(bn, length // tm, nj),
        in_specs=in_specs,
        out_specs=out_specs,
        out_shape=out_shape,
        scratch_shapes=[pltpu.VMEM((tm, d), BF16)],
        compiler_params=_params(),
        name="in_proj_latent" if latent else "in_proj_context",
    )(*args)


def _dwconv(x, w, b):
    n = x.shape[0]
    row = lax.broadcasted_iota(jnp.int32, x.shape, 0)
    y = b + x * w[2:3]
    y = y + jnp.where(row >= 2, pltpu.roll(x, 2, 0), 0.0) * w[0:1]
    y = y + jnp.where(row >= 1, pltpu.roll(x, 1, 0), 0.0) * w[1:2]
    y = y + jnp.where(row < n - 1, pltpu.roll(x, n - 1, 0), 0.0) * w[3:4]
    return y


def _scan_chunk(a, b, carry, reverse):
    tc = a.shape[0]
    nv = tc // SUBLANES
    a3 = a.reshape(nv, SUBLANES, LANES)
    b3 = b.reshape(nv, SUBLANES, LANES)
    row = lax.broadcasted_iota(jnp.int32, a3.shape, 1)
    step = 1
    while step < SUBLANES:
        if reverse:
            shift, valid = SUBLANES - step, row < SUBLANES - step
        else:
            shift, valid = step, row >= step
        a_sh = pltpu.roll(a3, shift, 1)
        b_sh = pltpu.roll(b3, shift, 1)
        b3 = jnp.where(valid, a3 * b_sh + b3, b3)
        a3 = jnp.where(valid, a3 * a_sh, a3)
        step *= 2
    last = 0 if reverse else SUBLANES - 1
    a_last = jnp.broadcast_to(a3[:, last:last + 1, :], a3.shape)
    b_last = jnp.broadcast_to(b3[:, last:last + 1, :], b3.shape)
    hs = [None] * nv
    for v in (range(nv - 1, -1, -1) if reverse else range(nv)):
        hs[v] = b3[v] + a3[v] * carry
        carry = b_last[v] + a_last[v] * carry
    return hs, carry


def _rglru_kernel(xl_ref, xc_ref, gl_ref, cw_ref, cb_ref, w_ref, bias_ref, lam_ref, o_ref,
                  xpad, xconv_c, xcs, hloc_f, aloc_f, hloc_b, aloc_b, hsum):
    cw = cw_ref[...]
    cb = cb_ref[...]
    length = xl_ref.shape[1]
    tseg = length // SUBLANES
    pitch = tseg + SEG_PAD
    n_j = tseg // SCAN_TJ
    n_ctx = xc_ref.shape[1] // SCAN_TC

    xconv_c[...] = _dwconv(xc_ref[0], cw, cb)

    pad_zeros = jnp.zeros((SEG_PAD, LANES), F32)
    xpad[0:SEG_PAD, :] = pad_zeros
    for s in range(SUBLANES):
        base = SEG_PAD + s * pitch
        xpad[base:base + tseg, :] = xl_ref[0, s * tseg:(s + 1) * tseg, :]
        pad = base + tseg
        xpad[pad:pad + SEG_PAD, :] = pad_zeros
        if s + 1 < SUBLANES:
            xpad[pad:pad + 1, :] = xl_ref[0, (s + 1) * tseg:(s + 1) * tseg + 1, :]
        xpad[pad + SEG_PAD - 2:pad + SEG_PAD, :] = xl_ref[0, (s + 1) * tseg - 2:(s + 1) * tseg, :]

    taps = [jnp.broadcast_to(cw[t:t + 1], (SUBLANES, LANES)) for t in range(CONV_W)]
    bias_rows = jnp.broadcast_to(cb, (SUBLANES, LANES))

    def conv_step(kk, _):
        j0 = kk * SCAN_TJ
        xs = [xpad[pl.ds(SEG_PAD + j0 - CONV_W // 2 + t, SUBLANES, stride=pitch), :]
              for t in range(SCAN_TJ + CONV_W - 1)]
        for jj in range(SCAN_TJ):
            y = bias_rows
            for t in range(CONV_W):
                y = y + xs[jj + t] * taps[t]
            xcs[pl.ds(pl.multiple_of((j0 + jj) * SUBLANES, SUBLANES), SUBLANES), :] = y
        return 0

    lax.fori_loop(0, n_j, conv_step, 0)

    def gates(xc, d):
        z = jnp.dot(xc.astype(BF16), w_ref[d, 0], preferred_element_type=F32) + bias_ref[d, 0]
        t_r = jnp.tanh(0.5 * z[:, :RNN_BLOCK_W])
        i = 0.5 * jnp.tanh(0.5 * z[:, RNN_BLOCK_W:]) + 0.5
        lam = lam_ref[d, 0]
        softplus_neg_lam = jnp.maximum(-lam, 0.0) + jnp.log1p(jnp.exp(-jnp.abs(lam)))
        half_rate = (-0.5 * RG_C) * softplus_neg_lam
        log_a = half_rate * t_r + half_rate
        a = jnp.exp(log_a)
        th = jnp.tanh(log_a)
        b = jnp.sqrt(-2.0 * th / (1.0 - th)) * (i * xc)
        return a, b

    ctx_state = []
    for d, reverse in ((0, False), (1, True)):
        def ctx_step(kk, carry, d=d, reverse=reverse):
            k = (n_ctx - 1 - kk) if reverse else kk
            r0 = pl.multiple_of(k * SCAN_TC, SCAN_TC)
            a, b = gates(xconv_c[pl.ds(r0, SCAN_TC), :], d)
            _, carry = _scan_chunk(a, b, carry, reverse)
            return carry

        ctx_state.append(lax.fori_loop(0, n_ctx, ctx_step, jnp.zeros((SUBLANES, LANES), F32)))

    def local_step(kk, carry):
        new = []
        for d, (h, acc_a) in enumerate(carry):
            reverse = d == 1
            j0 = ((n_j - 1 - kk) if reverse else kk) * SCAN_TJ
            src = pl.ds(pl.multiple_of(j0 * SUBLANES, SCAN_TJ * SUBLANES), SCAN_TJ * SUBLANES)
            a, b = gates(xcs[src, :], d)
            hloc, aloc = (hloc_b, aloc_b) if reverse else (hloc_f, aloc_f)
            for jj in (range(SCAN_TJ - 1, -1, -1) if reverse else range(SCAN_TJ)):
                rows = slice(jj * SUBLANES, (jj + 1) * SUBLANES)
                h = a[rows] * h + b[rows]
                acc_a = a[rows] * acc_a
                dst = pl.ds(pl.multiple_of((j0 + jj) * SUBLANES, SUBLANES), SUBLANES)
                hloc[dst, :] = h
                aloc[dst, :] = acc_a
            new.append((h, acc_a))
        return tuple(new)

    zeros = jnp.zeros((SUBLANES, LANES), F32)
    ones = jnp.ones((SUBLANES, LANES), F32)
    (h_f, a_f), (h_b, a_b) = lax.fori_loop(0, n_j, local_step, ((zeros, ones), (zeros, ones)))

    def entry_states(h_end, a_end, c_in, reverse):
        row = lax.broadcasted_iota(jnp.int32, (SUBLANES, LANES), 0)
        first = SUBLANES - 1 if reverse else 0
        c = c_in[0:1]
        out = jnp.where(row == first, jnp.broadcast_to(c, (SUBLANES, LANES)), 0.0)
        for k in range(1, SUBLANES):
            s = first - k if reverse else first + k
            prev = s + 1 if reverse else s - 1
            c = h_end[prev:prev + 1] + a_end[prev:prev + 1] * c
            out = jnp.where(row == s, jnp.broadcast_to(c, (SUBLANES, LANES)), out)
        return out

    c_f = entry_states(h_f, a_f, ctx_state[0], False)
    c_b = entry_states(h_b, a_b, ctx_state[1], True)

    def fix_step(kk, _):
        j0 = kk * SCAN_TJ
        for jj in range(SCAN_TJ):
            src = pl.ds(pl.multiple_of((j0 + jj) * SUBLANES, SUBLANES), SUBLANES)
            h = (hloc_f[src, :] + aloc_f[src, :] * c_f) + (hloc_b[src, :] + aloc_b[src, :] * c_b)
            hsum[pl.ds(j0 + jj, SUBLANES, stride=pitch), :] = h
        return 0

    lax.fori_loop(0, n_j, fix_step, 0)

    for s in range(SUBLANES):
        for r0 in range(0, tseg, SCAN_TC):
            src = slice(s * pitch + r0, s * pitch + r0 + SCAN_TC)
            dst = slice(s * tseg + r0, s * tseg + r0 + SCAN_TC)
            o_ref[0, dst, :] = (hsum[src, :] * _gelu(gl_ref[0, dst, :])).astype(BF16)


def _rglru(xl, xc, gl, conv_w, conv_b, w_gate, b_gate, lam):
    bn, s, _ = xl.shape
    ctx_len = xc.shape[1]
    bw = RNN_BLOCK_W
    padded = s + SUBLANES * SEG_PAD
    col = lambda b, n: (b, 0, n)
    return pl.pallas_call(
        _rglru_kernel,
        grid=(bn, RNN_BLOCKS),
        in_specs=[
            pl.BlockSpec((1, s, bw), col),
            pl.BlockSpec((1, ctx_len, bw), col),
            pl.BlockSpec((1, s, bw), col),
            pl.BlockSpec((CONV_W, bw), lambda b, n: (0, n)),
            pl.BlockSpec((1, bw), lambda b, n: (0, n)),
            pl.BlockSpec((2, 1, bw, 2 * bw), lambda b, n: (0, n, 0, 0)),
            pl.BlockSpec((2, 1, 1, 2 * bw), lambda b, n: (0, n, 0, 0)),
            pl.BlockSpec((2, 1, 1, bw), lambda b, n: (0, n, 0, 0)),
        ],
        out_specs=pl.BlockSpec((1, s, bw), col),
        out_shape=jax.ShapeDtypeStruct((bn, s, D_RNN), BF16),
        scratch_shapes=[
            pltpu.VMEM((SEG_PAD + padded, bw), F32),
            pltpu.VMEM((ctx_len, bw), F32),
            pltpu.VMEM((s, bw), F32),
            pltpu.VMEM((s, bw), F32), pltpu.VMEM((s, bw), F32),
            pltpu.VMEM((s, bw), F32), pltpu.VMEM((s, bw), F32),
            pltpu.VMEM((padded, bw), F32),
        ],
        compiler_params=_params(),
        name="rglru_bidirectional",
    )(xl, xc, gl, conv_w, conv_b, w_gate, b_gate, lam)


def _attn_kernel(q_ref, kc_ref, kl_ref, vc_ref, vl_ref, o_ref, k_scr, v_scr):
    n_ctx = kc_ref.shape[1]

    @pl.when(pl.program_id(2) == 0)
    def _():
        k_scr[0:n_ctx, :] = kc_ref[0]
        k_scr[n_ctx:, :] = kl_ref[0]
        v_scr[0:n_ctx, 0:HEAD_DIM] = vc_ref[0]
        v_scr[n_ctx:, 0:HEAD_DIM] = vl_ref[0]
        v_scr[:, HEAD_DIM:] = jnp.ones((v_scr.shape[0], HEAD_DIM), BF16)

    k = k_scr[...]
    v = v_scr[...]
    c = (HEAD_DIM ** -0.5) * 1.4426950408889634
    for r0 in range(0, q_ref.shape[1], ATTN_SUB):
        rows = slice(r0, r0 + ATTN_SUB)
        for hh in range(N_HEADS // N_KV_HEADS):
            sl = slice(hh * HEAD_DIM, (hh + 1) * HEAD_DIM)
            s = lax.dot_general(q_ref[0, rows, sl], k, (((1,), (1,)), ((), ())), preferred_element_type=F32)
            m = jnp.max(s, axis=-1, keepdims=True)
            p = jnp.exp2((s - m) * c).astype(BF16)
            ov = jnp.dot(p, v, preferred_element_type=F32)
            o_ref[0, rows, sl] = (ov[:, :HEAD_DIM] / ov[:, HEAD_DIM:HEAD_DIM + 1]).astype(BF16)


def _attention(q, kc, kl, vc, vl):
    bn, s, _ = q.shape
    n_ctx, n_lat = kc.shape[1], kl.shape[1]
    t = n_ctx + n_lat
    gw = ATTN_W // N_KV_HEADS
    kv_spec = lambda n: pl.BlockSpec((1, n, HEAD_DIM), lambda b, h, i: (b, 0, h))
    return pl.pallas_call(
        _attn_kernel,
        grid=(bn, N_KV_HEADS, s // ATTN_TQ),
        in_specs=[
            pl.BlockSpec((1, ATTN_TQ, gw), lambda b, h, i: (b, i, h)),
            kv_spec(n_ctx), kv_spec(n_lat), kv_spec(n_ctx), kv_spec(n_lat),
        ],
        out_specs=pl.BlockSpec((1, ATTN_TQ, gw), lambda b, h, i: (b, i, h)),
        out_shape=jax.ShapeDtypeStruct((bn, s, ATTN_W), BF16),
        scratch_shapes=[pltpu.VMEM((t, HEAD_DIM), BF16), pltpu.VMEM((t, 2 * HEAD_DIM), BF16)],
        compiler_params=_params(),
        name="gqa_attention",
    )(q, kc, kl, vc, vl)


def _residual_and_next(x, mix, gate, g_post, g_next, shift_next, scale_next):
    x1 = x + gate * _rms(mix, g_post)
    h = _rms(x1, g_next) * (1.0 + scale_next) + shift_next
    return x1, h.astype(BF16)


def _out_proj_kernel(a1_ref, a2_ref, w_ref, x_ref, gate_ref, g1_ref, g2_ref, sh_ref, sc_ref, x1_ref, h_ref):
    ka = a1_ref.shape[2]
    mix = jnp.dot(a1_ref[0], w_ref[0:ka, :], preferred_element_type=F32)
    mix = mix + jnp.dot(a2_ref[0], w_ref[ka:, :], preferred_element_type=F32)
    x1, h = _residual_and_next(x_ref[0], mix, gate_ref[0], g1_ref[...], g2_ref[...], sh_ref[0], sc_ref[0])
    x1_ref[0] = x1
    h_ref[0] = h


def _out_proj(a1, a2, w, x, gate, g1, g2, shift, scale):
    bn, s, d = x.shape
    tm = OUT_TM
    row = lambda b, i: (b, i, 0)
    mod = lambda b, i: (b, 0, 0)
    vec = lambda b, i: (0, 0)
    return pl.pallas_call(
        _out_proj_kernel,
        grid=(bn, s // tm),
        in_specs=[
            pl.BlockSpec((1, tm, a1.shape[2]), row),
            pl.BlockSpec((1, tm, a2.shape[2]), row),
            pl.BlockSpec(w.shape, vec),
            pl.BlockSpec((1, tm, d), row),
            pl.BlockSpec((1, 1, d), mod),
            pl.BlockSpec((1, d), vec),
            pl.BlockSpec((1, d), vec),
            pl.BlockSpec((1, 1, d), mod),
            pl.BlockSpec((1, 1, d), mod),
        ],
        out_specs=[pl.BlockSpec((1, tm, d), row), pl.BlockSpec((1, tm, d), row)],
        out_shape=[jax.ShapeDtypeStruct((bn, s, d), F32), jax.ShapeDtypeStruct((bn, s, d), BF16)],
        compiler_params=_params(),
        name="out_proj_residual",
    )(a1, a2, w, x, gate, g1, g2, shift, scale)


def _ffn_kernel(*refs, has_next, na):
    if has_next:
        (h_ref, w1_ref, w2_ref, x_ref, gate_ref, g3_ref, gn_ref, sh_ref, sc_ref,
         x2_ref, hn_ref, t_scr) = refs
    else:
        h_ref, w1_ref, w2_ref, x_ref, gate_ref, g3_ref, x2_ref, t_scr = refs
    j = pl.program_id(2)
    ta = t_scr.shape[2]
    tb = w2_ref.shape[1]

    @pl.when(j < na)
    def _():
        t = jnp.dot(h_ref[0], w1_ref[...], preferred_element_type=F32)
        t = jnp.maximum(t, 0.0)
        t_scr[j] = (t * t).astype(BF16)

    @pl.when(j >= na)
    def _():
        y = jnp.dot(t_scr[0], w2_ref[0:ta, :], preferred_element_type=F32)
        for a in range(1, na):
            y = y + jnp.dot(t_scr[a], w2_ref[a * ta:(a + 1) * ta, :], preferred_element_type=F32)
        col = pl.multiple_of((j - na) * tb, tb)
        x2_ref[0, :, pl.ds(col, tb)] = y

    @pl.when(j == pl.num_programs(2) - 1)
    def _():
        if has_next:
            x2, hn = _residual_and_next(x_ref[0], x2_ref[0], gate_ref[0], g3_ref[...],
                                        gn_ref[...], sh_ref[0], sc_ref[0])
            x2_ref[0] = x2
            hn_ref[0] = hn
        else:
            x2_ref[0] = x_ref[0] + gate_ref[0] * _rms(x2_ref[0], g3_ref[...])


def _ffn(h, w1, w2, x, gate, g3, nxt=None):
    bn, s, d = x.shape
    dff = w1.shape[1]
    tm, ta, tb = FFN_TM, FFN_TA, FFN_TB
    na, nb = dff // ta, d // tb
    row = lambda b, i, j: (b, i, 0)
    mod = lambda b, i, j: (b, 0, 0)
    vec = lambda b, i, j: (0, 0)
    in_specs = [
        pl.BlockSpec((1, tm, d), row),
        pl.BlockSpec((d, ta), lambda b, i, j: (0, jnp.minimum(j, na - 1))),
        pl.BlockSpec((dff, tb), lambda b, i, j: (0, jnp.maximum(j - na, 0))),
        pl.BlockSpec((1, tm, d), row),
        pl.BlockSpec((1, 1, d), mod),
        pl.BlockSpec((1, d), vec),
    ]
    args = [h, w1, w2, x, gate, g3]
    out_specs = [pl.BlockSpec((1, tm, d), row)]
    out_shape = [jax.ShapeDtypeStruct((bn, s, d), F32)]
    if nxt is not None:
        in_specs += [pl.BlockSpec((1, d), vec), pl.BlockSpec((1, 1, d), mod), pl.BlockSpec((1, 1, d), mod)]
        args += list(nxt)
        out_specs.append(pl.BlockSpec((1, tm, d), row))
        out_shape.append(jax.ShapeDtypeStruct((bn, s, d), BF16))
    return pl.pallas_call(
        functools.partial(_ffn_kernel, has_next=nxt is not None, na=na),
        grid=(bn, s // tm, na + nb),
        in_specs=in_specs,
        out_specs=out_specs,
        out_shape=out_shape,
        scratch_shapes=[pltpu.VMEM((na, tm, ta), BF16)],
        compiler_params=_params(),
        name="sq_relu_mlp",
    )(*args)


def _gm_in_kernel(h_ref, w_ref, b_ref, z_ref):
    z = jnp.dot(h_ref[0], w_ref[...], preferred_element_type=F32) + b_ref[...]
    z_ref[0] = _gelu(z)


def _gm_in(h, w, b):
    bn, s, d = h.shape
    n = w.shape[1]
    return pl.pallas_call(
        _gm_in_kernel,
        grid=(bn, s // GM_TM, n // GM_TN),
        in_specs=[
            pl.BlockSpec((1, GM_TM, d), lambda b_, i, j: (b_, i, 0)),
            pl.BlockSpec((d, GM_TN), lambda b_, i, j: (0, j)),
            pl.BlockSpec((1, GM_TN), lambda b_, i, j: (0, j)),
        ],
        out_specs=pl.BlockSpec((1, GM_TM, GM_TN), lambda b_, i, j: (b_, i, j)),
        out_shape=jax.ShapeDtypeStruct((bn, s, n), F32),
        compiler_params=_params(),
        name="gmlp_in_proj",
    )(h, w, b)


def _gm_out_kernel(u_ref, v_ref, vg_ref, vb_ref, wsp_ref, bsp_ref, w_ref, x_ref, gate_ref, g1_ref, g2_ref,
                   sh_ref, sc_ref, x1_ref, h_ref, vn_scr, gated_scr):
    v = v_ref[0]
    mu = jnp.mean(v, axis=-1, keepdims=True)
    vc = v - mu
    var = jnp.mean(vc * vc, axis=-1, keepdims=True)
    vn_scr[...] = (vc * lax.rsqrt(var + EPS) * vg_ref[...] + vb_ref[...]).astype(BF16)
    for c in range(v.shape[0] // CHUNK):
        rows = slice(c * CHUNK, (c + 1) * CHUNK)
        for g in range(GM_GROUPS):
            cols = slice(g * GM_GROUP_W, (g + 1) * GM_GROUP_W)
            sv = jnp.dot(wsp_ref[g], vn_scr[rows, cols], preferred_element_type=F32) + bsp_ref[:, g:g + 1]
            gated_scr[rows, cols] = (u_ref[0, rows, cols] * sv).astype(BF16)
    mix = jnp.dot(gated_scr[...], w_ref[...], preferred_element_type=F32)
    x1, h = _residual_and_next(x_ref[0], mix, gate_ref[0], g1_ref[...], g2_ref[...], sh_ref[0], sc_ref[0])
    x1_ref[0] = x1
    h_ref[0] = h


def _gm_out(z, vg, vb, wsp, bsp_t, w, x, gate, g1, g2, shift, scale):
    bn, s, d = x.shape
    dg = z.shape[2] // 2
    tm = GMO_TM
    row = lambda b, i: (b, i, 0)
    mod = lambda b, i: (b, 0, 0)
    vec = lambda b, i: (0, 0)
    return pl.pallas_call(
        _gm_out_kernel,
        grid=(bn, s // tm),
        in_specs=[
            pl.BlockSpec((1, tm, dg), lambda b, i: (b, i, 0)),
            pl.BlockSpec((1, tm, dg), lambda b, i: (b, i, 1)),
            pl.BlockSpec((1, dg), vec),
            pl.BlockSpec((1, dg), vec),
            pl.BlockSpec(wsp.shape, lambda b, i: (0, 0, 0)),
            pl.BlockSpec(bsp_t.shape, vec),
            pl.BlockSpec(w.shape, vec),
            pl.BlockSpec((1, tm, d), row),
            pl.BlockSpec((1, 1, d), mod),
            pl.BlockSpec((1, d), vec),
            pl.BlockSpec((1, d), vec),
            pl.BlockSpec((1, 1, d), mod),
            pl.BlockSpec((1, 1, d), mod),
        ],
        out_specs=[pl.BlockSpec((1, tm, d), row), pl.BlockSpec((1, tm, d), row)],
        out_shape=[jax.ShapeDtypeStruct((bn, s, d), F32), jax.ShapeDtypeStruct((bn, s, d), BF16)],
        scratch_shapes=[pltpu.VMEM((tm, dg), BF16), pltpu.VMEM((tm, dg), BF16)],
        compiler_params=_params(),
        name="gmlp_spatial_out_proj",
    )(z, z, vg, vb, wsp, bsp_t, w, x, gate, g1, g2, shift, scale)


def _rope_tables(n):
    t = jnp.arange(n)
    r_idx = (t // GRID_W).astype(F32)
    c_idx = (t % GRID_W).astype(F32)
    freqs = ROPE_THETA ** (-jnp.arange(ROPE_PAIRS, dtype=F32) / ROPE_PAIRS)
    ang_r = r_idx[:, None] * freqs
    ang_c = c_idx[:, None] * freqs
    cos = jnp.concatenate([jnp.cos(ang_r), jnp.cos(ang_r), jnp.cos(ang_c), jnp.cos(ang_c)], axis=-1)
    sin = jnp.concatenate([-jnp.sin(ang_r), jnp.sin(ang_r), -jnp.sin(ang_c), jnp.sin(ang_c)], axis=-1)
    return cos, sin


def kernel(x, c, ctx, c_ctx, w_mod, b_mod, norm_g, w_ff_in, w_ff_out, ar_w_in, ar_q_g, ar_k_g, ar_conv_w,
           ar_conv_b, ar_wa, ar_ba, ar_wx, ar_bx, ar_lambda, ar_w_out, gm_w_in, gm_b_in, gm_v_g, gm_v_b,
           gm_w_sp, gm_b_sp, gm_w_out):
    bn, s, d = x.shape
    depth = w_mod.shape[0]
    assert depth == 2, "layer pattern implemented for one attention/recurrent layer followed by one gMLP layer"

    rows = -(-(bn + 1) // SUBLANES) * SUBLANES
    cc = jnp.concatenate([c, c_ctx[None, :], jnp.zeros((rows - bn - 1, d), F32)], axis=0)
    mods = _modulation(cc, w_mod, b_mod)

    def lat_mod(layer, k):
        return mods[layer, :bn, k * d:(k + 1) * d].reshape(bn, 1, d)

    def ctx_mod(layer, k):
        return mods[layer, bn:bn + 1, k * d:(k + 1) * d].reshape(1, 1, d)

    g = norm_g.reshape(depth, 4, 1, d)

    w_in = ar_w_in[0].astype(BF16)
    qg = ar_q_g[0].reshape(1, HEAD_DIM)
    kg = ar_k_g[0].reshape(1, HEAD_DIM)
    cos, sin = _rope_tables(s)
    ql, kl, vl, xrl, grl = _in_proj(x, lat_mod(0, 0), lat_mod(0, 1), g[0, 0], w_in, qg, kg, cos, sin, latent=True)
    kc, vc, xrc = _in_proj(ctx, ctx_mod(0, 0), ctx_mod(0, 1), g[0, 0], w_in, qg, kg, None, None, latent=False)

    attn = _attention(ql, kc, kl, vc, vl)

    w_gate = jnp.concatenate([ar_wa[0], ar_wx[0]], axis=-1).astype(BF16)
    b_gate = jnp.concatenate([ar_ba[0].reshape(2, RNN_BLOCKS, 1, RNN_BLOCK_W),
                              ar_bx[0].reshape(2, RNN_BLOCKS, 1, RNN_BLOCK_W)], axis=-1)
    lam = ar_lambda[0].reshape(2, RNN_BLOCKS, 1, RNN_BLOCK_W)
    rnn = _rglru(xrl, xrc, grl, ar_conv_w[0], ar_conv_b[0].reshape(1, D_RNN), w_gate, b_gate, lam)

    x1, h = _out_proj(attn, rnn, ar_w_out[0].astype(BF16), x, lat_mod(0, 2), g[0, 1], g[0, 2],
                      lat_mod(0, 3), lat_mod(0, 4))
    x2, h = _ffn(h, w_ff_in[0].astype(BF16), w_ff_out[0].astype(BF16), x1, lat_mod(0, 5), g[0, 3],
                 nxt=(g[1, 0], lat_mod(1, 0), lat_mod(1, 1)))

    z = _gm_in(h, gm_w_in[0].astype(BF16), gm_b_in[0].reshape(1, -1))
    x3, h = _gm_out(z, gm_v_g[0].reshape(1, -1), gm_v_b[0].reshape(1, -1), gm_w_sp[0].astype(BF16),
                    gm_b_sp[0].T, gm_w_out[0].astype(BF16), x2, lat_mod(1, 2), g[1, 1], g[1, 2],
                    lat_mod(1, 3), lat_mod(1, 4))
    (x4,) = _ffn(h, w_ff_in[1].astype(BF16), w_ff_out[1].astype(BF16), x3, lat_mod(1, 5), g[1, 3])
    return x4
```

```python
import functools

import jax
import jax.numpy as jnp
from jax import lax
from jax.experimental import pallas as pl
from jax.experimental.pallas import tpu as pltpu

F32 = jnp.float32
BF16 = jnp.bfloat16

GRID_W = 64
N_HEADS = 8
N_KV_HEADS = 2
HEAD_DIM = 128
ATTN_W = N_HEADS * HEAD_DIM
KV_W = N_KV_HEADS * HEAD_DIM
ROPE_THETA = 10000.0
ROPE_PAIRS = HEAD_DIM // 4
D_RNN = 1024
RNN_BLOCKS = 8
RNN_BLOCK_W = D_RNN // RNN_BLOCKS
CONV_W = 4
RG_C = 8.0
GM_GROUPS = 16
GM_GROUP_W = 128
CHUNK = 128
EPS = 1e-6

LANES = 128
SUBLANES = 8
VMEM_LIMIT_BYTES = 56 * 1024 * 1024
FFN_VMEM_LIMIT_BYTES = 63 * 1024 * 1024

MOD_TN = 1024
PROJ_TN = 512
PROJ_TM = 512
SCAN_TC = 256
SCAN_TJ = 32
SEG_PAD = 8
ATTN_TQ = 512
ATTN_SUB = 256
OUT_TM = 512
EPI_SUB = 256
FFN_TM = 512
FFN_TA = 1024
FFN_TB = 512
GM_TM = 1024
GM_TN = 2048
GM_SUB = 1024
GMO_TM = 512
CAST_BLOCK_BYTES = 8 * 1024 * 1024


def _params():
    return pltpu.CompilerParams(vmem_limit_bytes=VMEM_LIMIT_BYTES)


def _rms(x, g):
    ms = jnp.mean(x * x, axis=-1, keepdims=True)
    return x * lax.rsqrt(ms + EPS) * g


def _sigmoid(x):
    return 1.0 / (1.0 + jnp.exp(-x))


def _gelu(x):
    c = 0.7978845608028654
    return 0.5 * x * (1.0 + jnp.tanh(c * (x + 0.044715 * (x * x * x))))


def _mod_kernel(cc_ref, w_ref, b_ref, o_ref):
    c = cc_ref[...]
    s = c * _sigmoid(c)
    o_ref[0] = jnp.dot(s.astype(BF16), w_ref[0].astype(BF16), preferred_element_type=F32) + b_ref[0]


def _modulation(cc, w_mod, b_mod):
    depth, d, n = w_mod.shape
    rows = cc.shape[0]
    return pl.pallas_call(
        _mod_kernel,
        grid=(depth, n // MOD_TN),
        in_specs=[
            pl.BlockSpec((rows, d), lambda l, j: (0, 0)),
            pl.BlockSpec((1, d, MOD_TN), lambda l, j: (l, 0, j)),
            pl.BlockSpec((1, 1, MOD_TN), lambda l, j: (l, 0, j)),
        ],
        out_specs=pl.BlockSpec((1, rows, MOD_TN), lambda l, j: (l, 0, j)),
        out_shape=jax.ShapeDtypeStruct((depth, rows, n), F32),
        compiler_params=_params(),
        name="adaln_modulation",
    )(cc, w_mod, b_mod.reshape(depth, 1, n))


def _cast_kernel(w_ref, o_ref):
    o_ref[...] = w_ref[...].astype(BF16)


def _to_bf16(w):
    layers, r, c = w.shape
    tr = min(r, CAST_BLOCK_BYTES // (c * 4))
    return pl.pallas_call(
        _cast_kernel,
        grid=(layers, r // tr),
        in_specs=[pl.BlockSpec((1, tr, c), lambda l, i: (l, i, 0))],
        out_specs=pl.BlockSpec((1, tr, c), lambda l, i: (l, i, 0)),
        out_shape=jax.ShapeDtypeStruct(w.shape, BF16),
        compiler_params=_params(),
        name="weights_to_bf16",
    )(w)


def _head_norm_rope(xh, gain, cos, sin):
    y = _rms(xh, gain)
    if cos is None:
        return y
    lane = lax.broadcasted_iota(jnp.int32, y.shape, 1)
    first_half = (lane % (2 * ROPE_PAIRS)) < ROPE_PAIRS
    partner = jnp.where(first_half,
                        pltpu.roll(y, HEAD_DIM - ROPE_PAIRS, 1),
                        pltpu.roll(y, ROPE_PAIRS, 1))
    return y * cos + partner * sin


def _in_proj_kernel(*refs, latent):
    if latent:
        (x_ref, sh_ref, sc_ref, g_ref, w_ref, qg_ref, kg_ref, cos_ref, sin_ref,
         q_ref, k_ref, v_ref, xr_ref, gr_ref, h_scr) = refs
        cos, sin = cos_ref[...], sin_ref[...]
    else:
        (x_ref, sh_ref, sc_ref, g_ref, w_ref, kg_ref, k_ref, v_ref, xr_ref, h_scr) = refs
        cos = sin = None
    h = _rms(x_ref[0], g_ref[...]) * (1.0 + sc_ref[0]) + sh_ref[0]
    h_scr[...] = h.astype(BF16)

    def project(c0, width):
        return jnp.dot(h_scr[...], w_ref[:, c0:c0 + width], preferred_element_type=F32)

    if latent:
        for c0 in range(0, ATTN_W, PROJ_TN):
            acc = project(c0, PROJ_TN)
            for hh in range(PROJ_TN // HEAD_DIM):
                sl = slice(hh * HEAD_DIM, (hh + 1) * HEAD_DIM)
                dst = slice(c0 + hh * HEAD_DIM, c0 + (hh + 1) * HEAD_DIM)
                q_ref[0, :, dst] = _head_norm_rope(acc[:, sl], qg_ref[...], cos, sin).astype(BF16)

    acc = project(ATTN_W, 2 * KV_W)
    for hh in range(N_KV_HEADS):
        sl = slice(hh * HEAD_DIM, (hh + 1) * HEAD_DIM)
        k_ref[0, :, sl] = _head_norm_rope(acc[:, sl], kg_ref[...], cos, sin).astype(BF16)
    v_ref[0] = acc[:, KV_W:2 * KV_W].astype(BF16)

    rnn0 = ATTN_W + 2 * KV_W
    for c0 in range(0, D_RNN, PROJ_TN):
        xr_ref[0, :, c0:c0 + PROJ_TN] = project(rnn0 + c0, PROJ_TN)
    if latent:
        for c0 in range(0, D_RNN, PROJ_TN):
            gr_ref[0, :, c0:c0 + PROJ_TN] = project(rnn0 + D_RNN + c0, PROJ_TN)


def _in_proj(x, shift, scale, g, w, qg, kg, cos, sin, *, latent):
    bn, length, d = x.shape
    tm = min(PROJ_TM, length)
    per_batch = shift.shape[0] > 1
    mod_map = (lambda b, i: (b, 0, 0)) if per_batch else (lambda b, i: (0, 0, 0))
    vec_map = lambda b, i: (0, 0)
    row = lambda b, i: (b, i, 0)

    in_specs = [
        pl.BlockSpec((1, tm, d), row),
        pl.BlockSpec((1, 1, d), mod_map),
        pl.BlockSpec((1, 1, d), mod_map),
        pl.BlockSpec((1, d), vec_map),
        pl.BlockSpec(w.shape, vec_map),
    ]
    args = [x, shift, scale, g, w]
    k_spec = pl.BlockSpec((1, tm, KV_W), row)
    xr_spec = pl.BlockSpec((1, tm, D_RNN), row)
    k_shape = jax.ShapeDtypeStruct((bn, length, KV_W), BF16)
    xr_shape = jax.ShapeDtypeStruct((bn, length, D_RNN), F32)
    if latent:
        in_specs += [pl.BlockSpec((1, HEAD_DIM), vec_map), pl.BlockSpec((1, HEAD_DIM), vec_map),
                     pl.BlockSpec((tm, HEAD_DIM), lambda b, i: (i, 0)),
                     pl.BlockSpec((tm, HEAD_DIM), lambda b, i: (i, 0))]
        args += [qg, kg, cos, sin]
        out_specs = [pl.BlockSpec((1, tm, ATTN_W), row), k_spec, k_spec, xr_spec, xr_spec]
        out_shape = [jax.ShapeDtypeStruct((bn, length, ATTN_W), BF16), k_shape, k_shape, xr_shape, xr_shape]
    else:
        in_specs += [pl.BlockSpec((1, HEAD_DIM), vec_map)]
        args += [kg]
        out_specs = [k_spec, k_spec, xr_spec]
        out_shape = [k_shape, k_shape, xr_shape]

    return pl.pallas_call(
        functools.partial(_in_proj_kernel, latent=latent),
        grid=(bn, length // tm),
        in_specs=in_specs,
        out_specs=out_specs,
        out_shape=out_shape,
        scratch_shapes=[pltpu.VMEM((tm, d), BF16)],
        compiler_params=_params(),
        name="in_proj_latent" if latent else "in_proj_context",
    )(*args)


def _dwconv(x, w, b):
    n = x.shape[0]
    row = lax.broadcasted_iota(jnp.int32, x.shape, 0)
    y = b + x * w[2:3]
    y = y + jnp.where(row >= 2, pltpu.roll(x, 2, 0), 0.0) * w[0:1]
    y = y + jnp.where(row >= 1, pltpu.roll(x, 1, 0), 0.0) * w[1:2]
    y = y + jnp.where(row < n - 1, pltpu.roll(x, n - 1, 0), 0.0) * w[3:4]
    return y


def _scan_chunk(a, b, carry, reverse):
    tc = a.shape[0]
    nv = tc // SUBLANES
    a3 = a.reshape(nv, SUBLANES, LANES)
    b3 = b.reshape(nv, SUBLANES, LANES)
    row = lax.broadcasted_iota(jnp.int32, a3.shape, 1)
    step = 1
    while step < SUBLANES:
        if reverse:
            shift, valid = SUBLANES - step, row < SUBLANES - step
        else:
            shift, valid = step, row >= step
        a_sh = pltpu.roll(a3, shift, 1)
        b_sh = pltpu.roll(b3, shift, 1)
        b3 = jnp.where(valid, a3 * b_sh + b3, b3)
        a3 = jnp.where(valid, a3 * a_sh, a3)
        step *= 2
    last = 0 if reverse else SUBLANES - 1
    a_last = jnp.broadcast_to(a3[:, last:last + 1, :], a3.shape)
    b_last = jnp.broadcast_to(b3[:, last:last + 1, :], b3.shape)
    hs = [None] * nv
    for v in (range(nv - 1, -1, -1) if reverse else range(nv)):
        hs[v] = b3[v] + a3[v] * carry
        carry = b_last[v] + a_last[v] * carry
    return hs, carry


def _rglru_kernel(xl_ref, xc_ref, gl_ref, cw_ref, cb_ref, w_ref, bias_ref, lam_ref, o_ref,
                  xpad, xconv_c, xcs, hloc_f, aloc_f, hloc_b, aloc_b, hsum):
    cw = cw_ref[...]
    cb = cb_ref[...]
    length = xl_ref.shape[1]
    tseg = length // SUBLANES
    pitch = tseg + SEG_PAD
    n_j = tseg // SCAN_TJ
    n_ctx = xc_ref.shape[1] // SCAN_TC

    xconv_c[...] = _dwconv(xc_ref[0], cw, cb)

    pad_zeros = jnp.zeros((SEG_PAD, LANES), F32)
    xpad[0:SEG_PAD, :] = pad_zeros
    for s in range(SUBLANES):
        base = SEG_PAD + s * pitch
        xpad[base:base + tseg, :] = xl_ref[0, s * tseg:(s + 1) * tseg, :]
        pad = base + tseg
        xpad[pad:pad + SEG_PAD, :] = pad_zeros
        if s + 1 < SUBLANES:
            xpad[pad:pad + 1, :] = xl_ref[0, (s + 1) * tseg:(s + 1) * tseg + 1, :]
        xpad[pad + SEG_PAD - 2:pad + SEG_PAD, :] = xl_ref[0, (s + 1) * tseg - 2:(s + 1) * tseg, :]

    taps = [jnp.broadcast_to(cw[t:t + 1], (SUBLANES, LANES)) for t in range(CONV_W)]
    bias_rows = jnp.broadcast_to(cb, (SUBLANES, LANES))

    def conv_step(kk, _):
        j0 = kk * SCAN_TJ
        xs = [xpad[pl.ds(SEG_PAD + j0 - CONV_W // 2 + t, SUBLANES, stride=pitch), :]
              for t in range(SCAN_TJ + CONV_W - 1)]
        for jj in range(SCAN_TJ):
            y = bias_rows
            for t in range(CONV_W):
                y = y + xs[jj + t] * taps[t]
            xcs[pl.ds(pl.multiple_of((j0 + jj) * SUBLANES, SUBLANES), SUBLANES), :] = y
        return 0

    lax.fori_loop(0, n_j, conv_step, 0)

    def gates(xc, d):
        z = jnp.dot(xc.astype(BF16), w_ref[d, 0], preferred_element_type=F32) + bias_ref[d, 0]
        t_r = jnp.tanh(0.5 * z[:, :RNN_BLOCK_W])
        i = 0.5 * jnp.tanh(0.5 * z[:, RNN_BLOCK_W:]) + 0.5
        lam = lam_ref[d, 0]
        softplus_neg_lam = jnp.maximum(-lam, 0.0) + jnp.log1p(jnp.exp(-jnp.abs(lam)))
        half_rate = (-0.5 * RG_C) * softplus_neg_lam
        log_a = half_rate * t_r + half_rate
        a = jnp.exp(log_a)
        th = jnp.tanh(log_a)
        b = jnp.sqrt(-2.0 * th / (1.0 - th)) * (i * xc)
        return a, b

    ctx_state = []
    for d, reverse in ((0, False), (1, True)):
        def ctx_step(kk, carry, d=d, reverse=reverse):
            k = (n_ctx - 1 - kk) if reverse else kk
            r0 = pl.multiple_of(k * SCAN_TC, SCAN_TC)
            a, b = gates(xconv_c[pl.ds(r0, SCAN_TC), :], d)
            _, carry = _scan_chunk(a, b, carry, reverse)
            return carry

        ctx_state.append(lax.fori_loop(0, n_ctx, ctx_step, jnp.zeros((SUBLANES, LANES), F32)))

    def local_step(kk, carry):
        new = []
        for d, (h, acc_a) in enumerate(carry):
            reverse = d == 1
            j0 = ((n_j - 1 - kk) if reverse else kk) * SCAN_TJ
            src = pl.ds(pl.multiple_of(j0 * SUBLANES, SCAN_TJ * SUBLANES), SCAN_TJ * SUBLANES)
            a, b = gates(xcs[src, :], d)
            hloc, aloc = (hloc_b, aloc_b) if reverse else (hloc_f, aloc_f)
            for jj in (range(SCAN_TJ - 1, -1, -1) if reverse else range(SCAN_TJ)):
                rows = slice(jj * SUBLANES, (jj + 1) * SUBLANES)
                h = a[rows] * h + b[rows]
                acc_a = a[rows] * acc_a
                dst = pl.ds(pl.multiple_of((j0 + jj) * SUBLANES, SUBLANES), SUBLANES)
                hloc[dst, :] = h
                aloc[dst, :] = acc_a
            new.append((h, acc_a))
        return tuple(new)

    zeros = jnp.zeros((SUBLANES, LANES), F32)
    ones = jnp.ones((SUBLANES, LANES), F32)
    (h_f, a_f), (h_b, a_b) = lax.fori_loop(0, n_j, local_step, ((zeros, ones), (zeros, ones)))

    def entry_states(h_end, a_end, c_in, reverse):
        row = lax.broadcasted_iota(jnp.int32, (SUBLANES, LANES), 0)
        first = SUBLANES - 1 if reverse else 0
        c = c_in[0:1]
        out = jnp.where(row == first, jnp.broadcast_to(c, (SUBLANES, LANES)), 0.0)
        for k in range(1, SUBLANES):
            s = first - k if reverse else first + k
            prev = s + 1 if reverse else s - 1
            c = h_end[prev:prev + 1] + a_end[prev:prev + 1] * c
            out = jnp.where(row == s, jnp.broadcast_to(c, (SUBLANES, LANES)), out)
        return out

    c_f = entry_states(h_f, a_f, ctx_state[0], False)
    c_b = entry_states(h_b, a_b, ctx_state[1], True)

    def fix_step(kk, _):
        j0 = kk * SCAN_TJ
        for jj in range(SCAN_TJ):
            src = pl.ds(pl.multiple_of((j0 + jj) * SUBLANES, SUBLANES), SUBLANES)
            h = (hloc_f[src, :] + aloc_f[src, :] * c_f) + (hloc_b[src, :] + aloc_b[src, :] * c_b)
            hsum[pl.ds(j0 + jj, SUBLANES, stride=pitch), :] = h
        return 0

    lax.fori_loop(0, n_j, fix_step, 0)

    for s in range(SUBLANES):
        for r0 in range(0, tseg, SCAN_TC):
            src = slice(s * pitch + r0, s * pitch + r0 + SCAN_TC)
            dst = slice(s * tseg + r0, s * tseg + r0 + SCAN_TC)
            o_ref[0, dst, :] = (hsum[src, :] * _gelu(gl_ref[0, dst, :])).astype(BF16)


def _rglru(xl, xc, gl, conv_w, conv_b, w_gate, b_gate, lam):
    bn, s, _ = xl.shape
    ctx_len = xc.shape[1]
    bw = RNN_BLOCK_W
    padded = s + SUBLANES * SEG_PAD
    col = lambda b, n: (b, 0, n)
    return pl.pallas_call(
        _rglru_kernel,
        grid=(bn, RNN_BLOCKS),
        in_specs=[
            pl.BlockSpec((1, s, bw), col),
            pl.BlockSpec((1, ctx_len, bw), col),
            pl.BlockSpec((1, s, bw), col),
            pl.BlockSpec((CONV_W, bw), lambda b, n: (0, n)),
            pl.BlockSpec((1, bw), lambda b, n: (0, n)),
            pl.BlockSpec((2, 1, bw, 2 * bw), lambda b, n: (0, n, 0, 0)),
            pl.BlockSpec((2, 1, 1, 2 * bw), lambda b, n: (0, n, 0, 0)),
            pl.BlockSpec((2, 1, 1, bw), lambda b, n: (0, n, 0, 0)),
        ],
        out_specs=pl.BlockSpec((1, s, bw), col),
        out_shape=jax.ShapeDtypeStruct((bn, s, D_RNN), BF16),
        scratch_shapes=[
            pltpu.VMEM((SEG_PAD + padded, bw), F32),
            pltpu.VMEM((ctx_len, bw), F32),
            pltpu.VMEM((s, bw), F32),
            pltpu.VMEM((s, bw), F32), pltpu.VMEM((s, bw), F32),
            pltpu.VMEM((s, bw), F32), pltpu.VMEM((s, bw), F32),
            pltpu.VMEM((padded, bw), F32),
        ],
        compiler_params=_params(),
        name="rglru_bidirectional",
    )(xl, xc, gl, conv_w, conv_b, w_gate, b_gate, lam)


def _attn_kernel(q_ref, kc_ref, kl_ref, vc_ref, vl_ref, o_ref, k_scr, v_scr):
    n_ctx = kc_ref.shape[1]

    @pl.when(pl.program_id(2) == 0)
    def _():
        k_scr[0:n_ctx, :] = kc_ref[0]
        k_scr[n_ctx:, :] = kl_ref[0]
        v_scr[0:n_ctx, 0:HEAD_DIM] = vc_ref[0]
        v_scr[n_ctx:, 0:HEAD_DIM] = vl_ref[0]
        v_scr[:, HEAD_DIM:] = jnp.ones((v_scr.shape[0], HEAD_DIM), BF16)

    k = k_scr[...]
    v = v_scr[...]
    c = (HEAD_DIM ** -0.5) * 1.4426950408889634
    for r0 in range(0, q_ref.shape[1], ATTN_SUB):
        rows = slice(r0, r0 + ATTN_SUB)
        for hh in range(N_HEADS // N_KV_HEADS):
            sl = slice(hh * HEAD_DIM, (hh + 1) * HEAD_DIM)
            s = lax.dot_general(q_ref[0, rows, sl], k, (((1,), (1,)), ((), ())), preferred_element_type=F32)
            m = jnp.max(s, axis=-1, keepdims=True)
            p = jnp.exp2((s - m) * c).astype(BF16)
            ov = jnp.dot(p, v, preferred_element_type=F32)
            o_ref[0, rows, sl] = (ov[:, :HEAD_DIM] / ov[:, HEAD_DIM:HEAD_DIM + 1]).astype(BF16)


def _attention(q, kc, kl, vc, vl):
    bn, s, _ = q.shape
    n_ctx, n_lat = kc.shape[1], kl.shape[1]
    t = n_ctx + n_lat
    gw = ATTN_W // N_KV_HEADS
    kv_spec = lambda n: pl.BlockSpec((1, n, HEAD_DIM), lambda b, h, i: (b, 0, h))
    return pl.pallas_call(
        _attn_kernel,
        grid=(bn, N_KV_HEADS, s // ATTN_TQ),
        in_specs=[
            pl.BlockSpec((1, ATTN_TQ, gw), lambda b, h, i: (b, i, h)),
            kv_spec(n_ctx), kv_spec(n_lat), kv_spec(n_ctx), kv_spec(n_lat),
        ],
        out_specs=pl.BlockSpec((1, ATTN_TQ, gw), lambda b, h, i: (b, i, h)),
        out_shape=jax.ShapeDtypeStruct((bn, s, ATTN_W), BF16),
        scratch_shapes=[pltpu.VMEM((t, HEAD_DIM), BF16), pltpu.VMEM((t, 2 * HEAD_DIM), BF16)],
        compiler_params=_params(),
        name="gqa_attention",
    )(q, kc, kl, vc, vl)


def _residual_and_next(x, mix, gate, g_post, g_next, shift_next, scale_next):
    x1 = x + gate * _rms(mix, g_post)
    h = _rms(x1, g_next) * (1.0 + scale_next) + shift_next
    return x1, h.astype(BF16)


def _out_proj_kernel(a1_ref, a2_ref, w_ref, x_ref, gate_ref, g1_ref, g2_ref, sh_ref, sc_ref, x1_ref, h_ref):
    ka = a1_ref.shape[2]
    for r0 in range(0, x_ref.shape[1], EPI_SUB):
        rows = slice(r0, r0 + EPI_SUB)
        mix = jnp.dot(a1_ref[0, rows, :], w_ref[0:ka, :], preferred_element_type=F32)
        mix = mix + jnp.dot(a2_ref[0, rows, :], w_ref[ka:, :], preferred_element_type=F32)
        x1, h = _residual_and_next(x_ref[0, rows, :], mix, gate_ref[0], g1_ref[...], g2_ref[...],
                                   sh_ref[0], sc_ref[0])
        x1_ref[0, rows, :] = x1
        h_ref[0, rows, :] = h


def _out_proj(a1, a2, w, x, gate, g1, g2, shift, scale):
    bn, s, d = x.shape
    tm = OUT_TM
    row = lambda b, i: (b, i, 0)
    mod = lambda b, i: (b, 0, 0)
    vec = lambda b, i: (0, 0)
    return pl.pallas_call(
        _out_proj_kernel,
        grid=(bn, s // tm),
        in_specs=[
            pl.BlockSpec((1, tm, a1.shape[2]), row),
            pl.BlockSpec((1, tm, a2.shape[2]), row),
            pl.BlockSpec(w.shape, vec),
            pl.BlockSpec((1, tm, d), row),
            pl.BlockSpec((1, 1, d), mod),
            pl.BlockSpec((1, d), vec),
            pl.BlockSpec((1, d), vec),
            pl.BlockSpec((1, 1, d), mod),
            pl.BlockSpec((1, 1, d), mod),
        ],
        out_specs=[pl.BlockSpec((1, tm, d), row), pl.BlockSpec((1, tm, d), row)],
        out_shape=[jax.ShapeDtypeStruct((bn, s, d), F32), jax.ShapeDtypeStruct((bn, s, d), BF16)],
        compiler_params=_params(),
        name="out_proj_residual",
    )(a1, a2, w, x, gate, g1, g2, shift, scale)


def _ffn_kernel(*refs, has_next, na):
    if has_next:
        (h_ref, w1_ref, w2_ref, x_ref, gate_ref, g3_ref, gn_ref, sh_ref, sc_ref,
         x2_ref, hn_ref, t_scr) = refs
    else:
        h_ref, w1_ref, w2_ref, x_ref, gate_ref, g3_ref, x2_ref, t_scr = refs
    j = pl.program_id(2)
    ta = t_scr.shape[2]
    tb = w2_ref.shape[1]

    @pl.when(j < na)
    def _():
        t = jnp.dot(h_ref[0], w1_ref[...], preferred_element_type=F32)
        t = jnp.maximum(t, 0.0)
        t_scr[j] = (t * t).astype(BF16)

    @pl.when(j >= na)
    def _():
        y = jnp.dot(t_scr[0], w2_ref[0:ta, :], preferred_element_type=F32)
        for a in range(1, na):
            y = y + jnp.dot(t_scr[a], w2_ref[a * ta:(a + 1) * ta, :], preferred_element_type=F32)
        col = pl.multiple_of((j - na) * tb, tb)
        x2_ref[0, :, pl.ds(col, tb)] = y

    @pl.when(j == pl.num_programs(2) - 1)
    def _():
        if has_next:
            x2, hn = _residual_and_next(x_ref[0], x2_ref[0], gate_ref[0], g3_ref[...],
                                        gn_ref[...], sh_ref[0], sc_ref[0])
            x2_ref[0] = x2
            hn_ref[0] = hn
        else:
            x2_ref[0] = x_ref[0] + gate_ref[0] * _rms(x2_ref[0], g3_ref[...])


def _ffn(h, w1, w2, layer, x, gate, g3, nxt=None):
    bn, s, d = x.shape
    dff = w1.shape[2]
    tm, ta, tb = FFN_TM, FFN_TA, FFN_TB
    na, nb = dff // ta, d // tb
    row = lambda b, i, j: (b, i, 0)
    mod = lambda b, i, j: (b, 0, 0)
    vec = lambda b, i, j: (0, 0)
    in_specs = [
        pl.BlockSpec((1, tm, d), row),
        pl.BlockSpec((None, d, ta), lambda b, i, j: (layer, 0, jnp.minimum(j, na - 1))),
        pl.BlockSpec((None, dff, tb), lambda b, i, j: (layer, 0, jnp.maximum(j - na, 0))),
        pl.BlockSpec((1, tm, d), row),
        pl.BlockSpec((1, 1, d), mod),
        pl.BlockSpec((1, d), vec),
    ]
    args = [h, w1, w2, x, gate, g3]
    out_specs = [pl.BlockSpec((1, tm, d), row)]
    out_shape = [jax.ShapeDtypeStruct((bn, s, d), F32)]
    if nxt is not None:
        in_specs += [pl.BlockSpec((1, d), vec), pl.BlockSpec((1, 1, d), mod), pl.BlockSpec((1, 1, d), mod)]
        args += list(nxt)
        out_specs.append(pl.BlockSpec((1, tm, d), row))
        out_shape.append(jax.ShapeDtypeStruct((bn, s, d), BF16))
    return pl.pallas_call(
        functools.partial(_ffn_kernel, has_next=nxt is not None, na=na),
        grid=(bn, s // tm, na + nb),
        in_specs=in_specs,
        out_specs=out_specs,
        out_shape=out_shape,
        scratch_shapes=[pltpu.VMEM((na, tm, ta), BF16)],
        compiler_params=pltpu.CompilerParams(vmem_limit_bytes=FFN_VMEM_LIMIT_BYTES),
        name="sq_relu_mlp",
    )(*args)


def _gm_in_kernel(h_ref, w_ref, b_ref, z_ref):
    for c0 in range(0, w_ref.shape[1], GM_SUB):
        cols = slice(c0, c0 + GM_SUB)
        z = jnp.dot(h_ref[0], w_ref[:, cols], preferred_element_type=F32) + b_ref[:, cols]
        z_ref[0, :, cols] = _gelu(z)


def _gm_in(h, w, b):
    bn, s, d = h.shape
    n = w.shape[1]
    return pl.pallas_call(
        _gm_in_kernel,
        grid=(bn, s // GM_TM, n // GM_TN),
        in_specs=[
            pl.BlockSpec((1, GM_TM, d), lambda b_, i, j: (b_, i, 0)),
            pl.BlockSpec((d, GM_TN), lambda b_, i, j: (0, j)),
            pl.BlockSpec((1, GM_TN), lambda b_, i, j: (0, j)),
        ],
        out_specs=pl.BlockSpec((1, GM_TM, GM_TN), lambda b_, i, j: (b_, i, j)),
        out_shape=jax.ShapeDtypeStruct((bn, s, n), F32),
        compiler_params=_params(),
        name="gmlp_in_proj",
    )(h, w, b)


def _gm_out_kernel(u_ref, v_ref, vg_ref, vb_ref, wsp_ref, bsp_ref, w_ref, x_ref, gate_ref, g1_ref, g2_ref,
                   sh_ref, sc_ref, x1_ref, h_ref, vn_scr, gated_scr):
    for r0 in range(0, x_ref.shape[1], EPI_SUB):
        sub = slice(r0, r0 + EPI_SUB)
        v = v_ref[0, sub, :]
        mu = jnp.mean(v, axis=-1, keepdims=True)
        vc = v - mu
        var = jnp.mean(vc * vc, axis=-1, keepdims=True)
        vn_scr[sub, :] = (vc * lax.rsqrt(var + EPS) * vg_ref[...] + vb_ref[...]).astype(BF16)
        for c0 in range(r0, r0 + EPI_SUB, CHUNK):
            rows = slice(c0, c0 + CHUNK)
            for g in range(GM_GROUPS):
                cols = slice(g * GM_GROUP_W, (g + 1) * GM_GROUP_W)
                sv = jnp.dot(wsp_ref[g], vn_scr[rows, cols], preferred_element_type=F32) + bsp_ref[:, g:g + 1]
                gated_scr[rows, cols] = (u_ref[0, rows, cols] * sv).astype(BF16)
        mix = jnp.dot(gated_scr[sub, :], w_ref[...], preferred_element_type=F32)
        x1, h = _residual_and_next(x_ref[0, sub, :], mix, gate_ref[0], g1_ref[...], g2_ref[...],
                                   sh_ref[0], sc_ref[0])
        x1_ref[0, sub, :] = x1
        h_ref[0, sub, :] = h


def _gm_out(z, vg, vb, wsp, bsp_t, w, x, gate, g1, g2, shift, scale):
    bn, s, d = x.shape
    dg = z.shape[2] // 2
    tm = GMO_TM
    row = lambda b, i: (b, i, 0)
    mod = lambda b, i: (b, 0, 0)
    vec = lambda b, i: (0, 0)
    return pl.pallas_call(
        _gm_out_kernel,
        grid=(bn, s // tm),
        in_specs=[
            pl.BlockSpec((1, tm, dg), lambda b, i: (b, i, 0)),
            pl.BlockSpec((1, tm, dg), lambda b, i: (b, i, 1)),
            pl.BlockSpec((1, dg), vec),
            pl.BlockSpec((1, dg), vec),
            pl.BlockSpec(wsp.shape, lambda b, i: (0, 0, 0), pipeline_mode=pl.Buffered(1)),
            pl.BlockSpec(bsp_t.shape, vec),
            pl.BlockSpec(w.shape, vec, pipeline_mode=pl.Buffered(1)),
            pl.BlockSpec((1, tm, d), row),
            pl.BlockSpec((1, 1, d), mod),
            pl.BlockSpec((1, d), vec),
            pl.BlockSpec((1, d), vec),
            pl.BlockSpec((1, 1, d), mod),
            pl.BlockSpec((1, 1, d), mod),
        ],
        out_specs=[pl.BlockSpec((1, tm, d), row), pl.BlockSpec((1, tm, d), row)],
        out_shape=[jax.ShapeDtypeStruct((bn, s, d), F32), jax.ShapeDtypeStruct((bn, s, d), BF16)],
        scratch_shapes=[pltpu.VMEM((tm, dg), BF16), pltpu.VMEM((tm, dg), BF16)],
        compiler_params=_params(),
        name="gmlp_spatial_out_proj",
    )(z, z, vg, vb, wsp, bsp_t, w, x, gate, g1, g2, shift, scale)


def _rope_tables(n):
    t = jnp.arange(n)
    r_idx = (t // GRID_W).astype(F32)
    c_idx = (t % GRID_W).astype(F32)
    freqs = ROPE_THETA ** (-jnp.arange(ROPE_PAIRS, dtype=F32) / ROPE_PAIRS)
    ang_r = r_idx[:, None] * freqs
    ang_c = c_idx[:, None] * freqs
    cos = jnp.concatenate([jnp.cos(ang_r), jnp.cos(ang_r), jnp.cos(ang_c), jnp.cos(ang_c)], axis=-1)
    sin = jnp.concatenate([-jnp.sin(ang_r), jnp.sin(ang_r), -jnp.sin(ang_c), jnp.sin(ang_c)], axis=-1)
    return cos, sin


def kernel(x, c, ctx, c_ctx, w_mod, b_mod, norm_g, w_ff_in, w_ff_out, ar_w_in, ar_q_g, ar_k_g, ar_conv_w,
           ar_conv_b, ar_wa, ar_ba, ar_wx, ar_bx, ar_lambda, ar_w_out, gm_w_in, gm_b_in, gm_v_g, gm_v_b,
           gm_w_sp, gm_b_sp, gm_w_out):
    bn, s, d = x.shape
    depth = w_mod.shape[0]
    assert depth == 2, "layer pattern implemented for one attention/recurrent layer followed by one gMLP layer"

    rows = -(-(bn + 1) // SUBLANES) * SUBLANES
    cc = jnp.concatenate([c, c_ctx[None, :], jnp.zeros((rows - bn - 1, d), F32)], axis=0)
    mods = _modulation(cc, w_mod, b_mod)

    def lat_mod(layer, k):
        return mods[layer, :bn, k * d:(k + 1) * d].reshape(bn, 1, d)

    def ctx_mod(layer, k):
        return mods[layer, bn:bn + 1, k * d:(k + 1) * d].reshape(1, 1, d)

    g = norm_g.reshape(depth, 4, 1, d)
    w_ff1 = _to_bf16(w_ff_in)
    w_ff2 = _to_bf16(w_ff_out)

    w_in = ar_w_in[0].astype(BF16)
    qg = ar_q_g[0].reshape(1, HEAD_DIM)
    kg = ar_k_g[0].reshape(1, HEAD_DIM)
    cos, sin = _rope_tables(s)
    ql, kl, vl, xrl, grl = _in_proj(x, lat_mod(0, 0), lat_mod(0, 1), g[0, 0], w_in, qg, kg, cos, sin, latent=True)
    kc, vc, xrc = _in_proj(ctx, ctx_mod(0, 0), ctx_mod(0, 1), g[0, 0], w_in, qg, kg, None, None, latent=False)

    attn = _attention(ql, kc, kl, vc, vl)

    w_gate = jnp.concatenate([ar_wa[0], ar_wx[0]], axis=-1).astype(BF16)
    b_gate = jnp.concatenate([ar_ba[0].reshape(2, RNN_BLOCKS, 1, RNN_BLOCK_W),
                              ar_bx[0].reshape(2, RNN_BLOCKS, 1, RNN_BLOCK_W)], axis=-1)
    lam = ar_lambda[0].reshape(2, RNN_BLOCKS, 1, RNN_BLOCK_W)
    rnn = _rglru(xrl, xrc, grl, ar_conv_w[0], ar_conv_b[0].reshape(1, D_RNN), w_gate, b_gate, lam)

    x1, h = _out_proj(attn, rnn, ar_w_out[0].astype(BF16), x, lat_mod(0, 2), g[0, 1], g[0, 2],
                      lat_mod(0, 3), lat_mod(0, 4))
    x2, h = _ffn(h, w_ff1, w_ff2, 0, x1, lat_mod(0, 5), g[0, 3], nxt=(g[1, 0], lat_mod(1, 0), lat_mod(1, 1)))

    z = _gm_in(h, gm_w_in[0].astype(BF16), gm_b_in[0].reshape(1, -1))
    x3, h = _gm_out(z, gm_v_g[0].reshape(1, -1), gm_v_b[0].reshape(1, -1), gm_w_sp[0].astype(BF16),
                    gm_b_sp[0].T, gm_w_out[0].astype(BF16), x2, lat_mod(1, 2), g[1, 1], g[1, 2],
                    lat_mod(1, 3), lat_mod(1, 4))
    (x4,) = _ffn(h, w_ff1, w_ff2, 1, x3, lat_mod(1, 5), g[1, 3])
    return x4
```

```python
import functools

import jax
import jax.numpy as jnp
from jax import lax
from jax.experimental import pallas as pl
from jax.experimental.pallas import tpu as pltpu

F32 = jnp.float32
BF16 = jnp.bfloat16

GRID_W = 64
N_HEADS = 8
N_KV_HEADS = 2
HEAD_DIM = 128
ATTN_W = N_HEADS * HEAD_DIM
KV_W = N_KV_HEADS * HEAD_DIM
ROPE_THETA = 10000.0
ROPE_PAIRS = HEAD_DIM // 4
D_RNN = 1024
RNN_BLOCKS = 8
RNN_BLOCK_W = D_RNN // RNN_BLOCKS
CONV_W = 4
RG_C = 8.0
GM_GROUPS = 16
GM_GROUP_W = 128
CHUNK = 128
EPS = 1e-6

LANES = 128
SUBLANES = 8
VMEM_LIMIT_BYTES = 56 * 1024 * 1024
FFN_VMEM_LIMIT_BYTES = 63 * 1024 * 1024

MOD_TN = 1024
PROJ_TN = 512
PROJ_TM = 512
SCAN_TC = 256
SCAN_TJ = 32
SEG_PAD = 8
ATTN_TQ = 1024
ATTN_SUB = 256
OUT_TM = 512
EPI_SUB = 256
FFN_TM = 512
FFN_TA = 1024
FFN_TB = 512
GM_TM = 1024
GM_TN = 2048
GM_SUB = 1024
GMO_TM = 512
CAST_BLOCK_BYTES = 8 * 1024 * 1024


def _params():
    return pltpu.CompilerParams(vmem_limit_bytes=VMEM_LIMIT_BYTES)


def _rms(x, g):
    ms = jnp.mean(x * x, axis=-1, keepdims=True)
    return x * lax.rsqrt(ms + EPS) * g


def _sigmoid(x):
    return 1.0 / (1.0 + jnp.exp(-x))


def _gelu(x):
    c = 0.7978845608028654
    return 0.5 * x * (1.0 + jnp.tanh(c * (x + 0.044715 * (x * x * x))))


def _mod_kernel(cc_ref, w_ref, b_ref, o_ref):
    c = cc_ref[...]
    s = c * _sigmoid(c)
    o_ref[0] = jnp.dot(s.astype(BF16), w_ref[0].astype(BF16), preferred_element_type=F32) + b_ref[0]


def _modulation(cc, w_mod, b_mod):
    depth, d, n = w_mod.shape
    rows = cc.shape[0]
    return pl.pallas_call(
        _mod_kernel,
        grid=(depth, n // MOD_TN),
        in_specs=[
            pl.BlockSpec((rows, d), lambda l, j: (0, 0)),
            pl.BlockSpec((1, d, MOD_TN), lambda l, j: (l, 0, j)),
            pl.BlockSpec((1, 1, MOD_TN), lambda l, j: (l, 0, j)),
        ],
        out_specs=pl.BlockSpec((1, rows, MOD_TN), lambda l, j: (l, 0, j)),
        out_shape=jax.ShapeDtypeStruct((depth, rows, n), F32),
        compiler_params=_params(),
        name="adaln_modulation",
    )(cc, w_mod, b_mod.reshape(depth, 1, n))


def _cast_kernel(w_ref, o_ref):
    tc = o_ref.shape[3]
    for a in range(o_ref.shape[1]):
        o_ref[0, a] = w_ref[0, :, a * tc:(a + 1) * tc].astype(BF16)


def _to_bf16_col_blocks(w, tc):
    layers, r, c = w.shape
    tr = min(r, CAST_BLOCK_BYTES // (c * 4))
    return pl.pallas_call(
        _cast_kernel,
        grid=(layers, r // tr),
        in_specs=[pl.BlockSpec((1, tr, c), lambda l, i: (l, i, 0))],
        out_specs=pl.BlockSpec((1, c // tc, tr, tc), lambda l, i: (l, 0, i, 0)),
        out_shape=jax.ShapeDtypeStruct((layers, c // tc, r, tc), BF16),
        compiler_params=_params(),
        name="weights_to_bf16",
    )(w)


def _head_norm_rope(xh, gain, cos, sin):
    y = _rms(xh, gain)
    if cos is None:
        return y
    lane = lax.broadcasted_iota(jnp.int32, y.shape, 1)
    first_half = (lane % (2 * ROPE_PAIRS)) < ROPE_PAIRS
    partner = jnp.where(first_half,
                        pltpu.roll(y, HEAD_DIM - ROPE_PAIRS, 1),
                        pltpu.roll(y, ROPE_PAIRS, 1))
    return y * cos + partner * sin


def _in_proj_kernel(*refs, latent):
    if latent:
        (x_ref, sh_ref, sc_ref, g_ref, w_ref, qg_ref, kg_ref, cos_ref, sin_ref,
         q_ref, k_ref, v_ref, xr_ref, gr_ref, h_scr) = refs
        cos, sin = cos_ref[...], sin_ref[...]
    else:
        (x_ref, sh_ref, sc_ref, g_ref, w_ref, kg_ref, k_ref, v_ref, xr_ref, h_scr) = refs
        cos = sin = None
    h = _rms(x_ref[0], g_ref[...]) * (1.0 + sc_ref[0]) + sh_ref[0]
    h_scr[...] = h.astype(BF16)

    def project(c0, width):
        return jnp.dot(h_scr[...], w_ref[:, c0:c0 + width], preferred_element_type=F32)

    if latent:
        for c0 in range(0, ATTN_W, PROJ_TN):
            acc = project(c0, PROJ_TN)
            for hh in range(PROJ_TN // HEAD_DIM):
                sl = slice(hh * HEAD_DIM, (hh + 1) * HEAD_DIM)
                dst = slice(c0 + hh * HEAD_DIM, c0 + (hh + 1) * HEAD_DIM)
                q_ref[0, :, dst] = _head_norm_rope(acc[:, sl], qg_ref[...], cos, sin).astype(BF16)

    acc = project(ATTN_W, 2 * KV_W)
    for hh in range(N_KV_HEADS):
        sl = slice(hh * HEAD_DIM, (hh + 1) * HEAD_DIM)
        k_ref[0, :, sl] = _head_norm_rope(acc[:, sl], kg_ref[...], cos, sin).astype(BF16)
    v_ref[0] = acc[:, KV_W:2 * KV_W].astype(BF16)

    rnn0 = ATTN_W + 2 * KV_W
    for c0 in range(0, D_RNN, PROJ_TN):
        xr_ref[0, :, c0:c0 + PROJ_TN] = project(rnn0 + c0, PROJ_TN)
    if latent:
        for c0 in range(0, D_RNN, PROJ_TN):
            gr_ref[0, :, c0:c0 + PROJ_TN] = project(rnn0 + D_RNN + c0, PROJ_TN)


def _in_proj(x, shift, scale, g, w, qg, kg, cos, sin, *, latent):
    bn, length, d = x.shape
    tm = min(PROJ_TM, length)
    per_batch = shift.shape[0] > 1
    mod_map = (lambda b, i: (b, 0, 0)) if per_batch else (lambda b, i: (0, 0, 0))
    vec_map = lambda b, i: (0, 0)
    row = lambda b, i: (b, i, 0)

    in_specs = [
        pl.BlockSpec((1, tm, d), row),
        pl.BlockSpec((1, 1, d), mod_map),
        pl.BlockSpec((1, 1, d), mod_map),
        pl.BlockSpec((1, d), vec_map),
        pl.BlockSpec(w.shape, vec_map),
    ]
    args = [x, shift, scale, g, w]
    k_spec = pl.BlockSpec((1, tm, KV_W), row)
    xr_spec = pl.BlockSpec((1, tm, D_RNN), row)
    k_shape = jax.ShapeDtypeStruct((bn, length, KV_W), BF16)
    xr_shape = jax.ShapeDtypeStruct((bn, length, D_RNN), F32)
    if latent:
        in_specs += [pl.BlockSpec((1, HEAD_DIM), vec_map), pl.BlockSpec((1, HEAD_DIM), vec_map),
                     pl.BlockSpec((tm, HEAD_DIM), lambda b, i: (i, 0)),
                     pl.BlockSpec((tm, HEAD_DIM), lambda b, i: (i, 0))]
        args += [qg, kg, cos, sin]
        out_specs = [pl.BlockSpec((1, tm, ATTN_W), row), k_spec, k_spec, xr_spec, xr_spec]
        out_shape = [jax.ShapeDtypeStruct((bn, length, ATTN_W), BF16), k_shape, k_shape, xr_shape, xr_shape]
    else:
        in_specs += [pl.BlockSpec((1, HEAD_DIM), vec_map)]
        args += [kg]
        out_specs = [k_spec, k_spec, xr_spec]
        out_shape = [k_shape, k_shape, xr_shape]

    return pl.pallas_call(
        functools.partial(_in_proj_kernel, latent=latent),
        grid=(bn, length // tm),
        in_specs=in_specs,
        out_specs=out_specs,
        out_shape=out_shape,
        scratch_shapes=[pltpu.VMEM((tm, d), BF16)],
        compiler_params=_params(),
        name="in_proj_latent" if latent else "in_proj_context",
    )(*args)


def _dwconv(x, w, b):
    n = x.shape[0]
    row = lax.broadcasted_iota(jnp.int32, x.shape, 0)
    y = b + x * w[2:3]
    y = y + jnp.where(row >= 2, pltpu.roll(x, 2, 0), 0.0) * w[0:1]
    y = y + jnp.where(row >= 1, pltpu.roll(x, 1, 0), 0.0) * w[1:2]
    y = y + jnp.where(row < n - 1, pltpu.roll(x, n - 1, 0), 0.0) * w[3:4]
    return y


def _scan_chunk(a, b, carry, reverse):
    tc = a.shape[0]
    nv = tc // SUBLANES
    a3 = a.reshape(nv, SUBLANES, LANES)
    b3 = b.reshape(nv, SUBLANES, LANES)
    row = lax.broadcasted_iota(jnp.int32, a3.shape, 1)
    step = 1
    while step < SUBLANES:
        if reverse:
            shift, valid = SUBLANES - step, row < SUBLANES - step
        else:
            shift, valid = step, row >= step
        a_sh = pltpu.roll(a3, shift, 1)
        b_sh = pltpu.roll(b3, shift, 1)
        b3 = jnp.where(valid, a3 * b_sh + b3, b3)
        a3 = jnp.where(valid, a3 * a_sh, a3)
        step *= 2
    last = 0 if reverse else SUBLANES - 1
    a_last = jnp.broadcast_to(a3[:, last:last + 1, :], a3.shape)
    b_last = jnp.broadcast_to(b3[:, last:last + 1, :], b3.shape)
    hs = [None] * nv
    for v in (range(nv - 1, -1, -1) if reverse else range(nv)):
        hs[v] = b3[v] + a3[v] * carry
        carry = b_last[v] + a_last[v] * carry
    return hs, carry


def _rglru_kernel(xl_ref, xc_ref, gl_ref, cw_ref, cb_ref, w_ref, bias_ref, lam_ref, o_ref,
                  xpad, xconv_c, xcs, hloc_f, aloc_f, hloc_b, aloc_b, hsum):
    cw = cw_ref[...]
    cb = cb_ref[...]
    length = xl_ref.shape[1]
    tseg = length // SUBLANES
    pitch = tseg + SEG_PAD
    n_j = tseg // SCAN_TJ
    n_ctx = xc_ref.shape[1] // SCAN_TC

    xconv_c[...] = _dwconv(xc_ref[0], cw, cb)

    pad_zeros = jnp.zeros((SEG_PAD, LANES), F32)
    xpad[0:SEG_PAD, :] = pad_zeros
    for s in range(SUBLANES):
        base = SEG_PAD + s * pitch
        xpad[base:base + tseg, :] = xl_ref[0, s * tseg:(s + 1) * tseg, :]
        pad = base + tseg
        xpad[pad:pad + SEG_PAD, :] = pad_zeros
        if s + 1 < SUBLANES:
            xpad[pad:pad + 1, :] = xl_ref[0, (s + 1) * tseg:(s + 1) * tseg + 1, :]
        xpad[pad + SEG_PAD - 2:pad + SEG_PAD, :] = xl_ref[0, (s + 1) * tseg - 2:(s + 1) * tseg, :]

    taps = [jnp.broadcast_to(cw[t:t + 1], (SUBLANES, LANES)) for t in range(CONV_W)]
    bias_rows = jnp.broadcast_to(cb, (SUBLANES, LANES))

    def conv_step(kk, _):
        j0 = kk * SCAN_TJ
        xs = [xpad[pl.ds(SEG_PAD + j0 - CONV_W // 2 + t, SUBLANES, stride=pitch), :]
              for t in range(SCAN_TJ + CONV_W - 1)]
        for jj in range(SCAN_TJ):
            y = bias_rows
            for t in range(CONV_W):
                y = y + xs[jj + t] * taps[t]
            xcs[pl.ds(pl.multiple_of((j0 + jj) * SUBLANES, SUBLANES), SUBLANES), :] = y
        return 0

    lax.fori_loop(0, n_j, conv_step, 0)

    def gates(xc, d):
        z = jnp.dot(xc.astype(BF16), w_ref[d, 0], preferred_element_type=F32) + bias_ref[d, 0]
        t_r = jnp.tanh(0.5 * z[:, :RNN_BLOCK_W])
        i = 0.5 * jnp.tanh(0.5 * z[:, RNN_BLOCK_W:]) + 0.5
        lam = lam_ref[d, 0]
        softplus_neg_lam = jnp.maximum(-lam, 0.0) + jnp.log1p(jnp.exp(-jnp.abs(lam)))
        half_rate = (-0.5 * RG_C) * softplus_neg_lam
        log_a = half_rate * t_r + half_rate
        a = jnp.exp(log_a)
        th = jnp.tanh(log_a)
        b = jnp.sqrt(-2.0 * th / (1.0 - th)) * (i * xc)
        return a, b

    ctx_state = []
    for d, reverse in ((0, False), (1, True)):
        def ctx_step(kk, carry, d=d, reverse=reverse):
            k = (n_ctx - 1 - kk) if reverse else kk
            r0 = pl.multiple_of(k * SCAN_TC, SCAN_TC)
            a, b = gates(xconv_c[pl.ds(r0, SCAN_TC), :], d)
            _, carry = _scan_chunk(a, b, carry, reverse)
            return carry

        ctx_state.append(lax.fori_loop(0, n_ctx, ctx_step, jnp.zeros((SUBLANES, LANES), F32)))

    def local_step(kk, carry):
        new = []
        for d, (h, acc_a) in enumerate(carry):
            reverse = d == 1
            j0 = ((n_j - 1 - kk) if reverse else kk) * SCAN_TJ
            src = pl.ds(pl.multiple_of(j0 * SUBLANES, SCAN_TJ * SUBLANES), SCAN_TJ * SUBLANES)
            a, b = gates(xcs[src, :], d)
            hloc, aloc = (hloc_b, aloc_b) if reverse else (hloc_f, aloc_f)
            for jj in (range(SCAN_TJ - 1, -1, -1) if reverse else range(SCAN_TJ)):
                rows = slice(jj * SUBLANES, (jj + 1) * SUBLANES)
                h = a[rows] * h + b[rows]
                acc_a = a[rows] * acc_a
                dst = pl.ds(pl.multiple_of((j0 + jj) * SUBLANES, SUBLANES), SUBLANES)
                hloc[dst, :] = h
                aloc[dst, :] = acc_a
            new.append((h, acc_a))
        return tuple(new)

    zeros = jnp.zeros((SUBLANES, LANES), F32)
    ones = jnp.ones((SUBLANES, LANES), F32)
    (h_f, a_f), (h_b, a_b) = lax.fori_loop(0, n_j, local_step, ((zeros, ones), (zeros, ones)))

    def entry_states(h_end, a_end, c_in, reverse):
        row = lax.broadcasted_iota(jnp.int32, (SUBLANES, LANES), 0)
        first = SUBLANES - 1 if reverse else 0
        c = c_in[0:1]
        out = jnp.where(row == first, jnp.broadcast_to(c, (SUBLANES, LANES)), 0.0)
        for k in range(1, SUBLANES):
            s = first - k if reverse else first + k
            prev = s + 1 if reverse else s - 1
            c = h_end[prev:prev + 1] + a_end[prev:prev + 1] * c
            out = jnp.where(row == s, jnp.broadcast_to(c, (SUBLANES, LANES)), out)
        return out

    c_f = entry_states(h_f, a_f, ctx_state[0], False)
    c_b = entry_states(h_b, a_b, ctx_state[1], True)

    def fix_step(kk, _):
        j0 = kk * SCAN_TJ
        for jj in range(SCAN_TJ):
            src = pl.ds(pl.multiple_of((j0 + jj) * SUBLANES, SUBLANES), SUBLANES)
            h = (hloc_f[src, :] + aloc_f[src, :] * c_f) + (hloc_b[src, :] + aloc_b[src, :] * c_b)
            hsum[pl.ds(j0 + jj, SUBLANES, stride=pitch), :] = h
        return 0

    lax.fori_loop(0, n_j, fix_step, 0)

    for s in range(SUBLANES):
        for r0 in range(0, tseg, SCAN_TC):
            src = slice(s * pitch + r0, s * pitch + r0 + SCAN_TC)
            dst = slice(s * tseg + r0, s * tseg + r0 + SCAN_TC)
            o_ref[0, dst, :] = (hsum[src, :] * _gelu(gl_ref[0, dst, :])).astype(BF16)


def _rglru(xl, xc, gl, conv_w, conv_b, w_gate, b_gate, lam):
    bn, s, _ = xl.shape
    ctx_len = xc.shape[1]
    bw = RNN_BLOCK_W
    padded = s + SUBLANES * SEG_PAD
    col = lambda b, n: (b, 0, n)
    return pl.pallas_call(
        _rglru_kernel,
        grid=(bn, RNN_BLOCKS),
        in_specs=[
            pl.BlockSpec((1, s, bw), col),
            pl.BlockSpec((1, ctx_len, bw), col),
            pl.BlockSpec((1, s, bw), col),
            pl.BlockSpec((CONV_W, bw), lambda b, n: (0, n)),
            pl.BlockSpec((1, bw), lambda b, n: (0, n)),
            pl.BlockSpec((2, 1, bw, 2 * bw), lambda b, n: (0, n, 0, 0)),
            pl.BlockSpec((2, 1, 1, 2 * bw), lambda b, n: (0, n, 0, 0)),
            pl.BlockSpec((2, 1, 1, bw), lambda b, n: (0, n, 0, 0)),
        ],
        out_specs=pl.BlockSpec((1, s, bw), col),
        out_shape=jax.ShapeDtypeStruct((bn, s, D_RNN), BF16),
        scratch_shapes=[
            pltpu.VMEM((SEG_PAD + padded, bw), F32),
            pltpu.VMEM((ctx_len, bw), F32),
            pltpu.VMEM((s, bw), F32),
            pltpu.VMEM((s, bw), F32), pltpu.VMEM((s, bw), F32),
            pltpu.VMEM((s, bw), F32), pltpu.VMEM((s, bw), F32),
            pltpu.VMEM((padded, bw), F32),
        ],
        compiler_params=_params(),
        name="rglru_bidirectional",
    )(xl, xc, gl, conv_w, conv_b, w_gate, b_gate, lam)


def _attn_kernel(q_ref, kc_ref, kl_ref, vc_ref, vl_ref, o_ref, k_scr, v_scr):
    n_ctx = kc_ref.shape[1]

    @pl.when(pl.program_id(2) == 0)
    def _():
        k_scr[0:n_ctx, :] = kc_ref[0]
        k_scr[n_ctx:, :] = kl_ref[0]
        v_scr[0:n_ctx, 0:HEAD_DIM] = vc_ref[0]
        v_scr[n_ctx:, 0:HEAD_DIM] = vl_ref[0]
        v_scr[:, HEAD_DIM:] = jnp.ones((v_scr.shape[0], HEAD_DIM), BF16)

    k = k_scr[...]
    v = v_scr[...]
    c = (HEAD_DIM ** -0.5) * 1.4426950408889634
    for r0 in range(0, q_ref.shape[1], ATTN_SUB):
        rows = slice(r0, r0 + ATTN_SUB)
        for hh in range(N_HEADS // N_KV_HEADS):
            sl = slice(hh * HEAD_DIM, (hh + 1) * HEAD_DIM)
            s = lax.dot_general(q_ref[0, rows, sl], k, (((1,), (1,)), ((), ())), preferred_element_type=F32)
            m = jnp.max(s, axis=-1, keepdims=True)
            p = jnp.exp2((s - m) * c).astype(BF16)
            ov = jnp.dot(p, v, preferred_element_type=F32)
            o_ref[0, rows, sl] = (ov[:, :HEAD_DIM] / ov[:, HEAD_DIM:HEAD_DIM + 1]).astype(BF16)


def _attention(q, kc, kl, vc, vl):
    bn, s, _ = q.shape
    n_ctx, n_lat = kc.shape[1], kl.shape[1]
    t = n_ctx + n_lat
    gw = ATTN_W // N_KV_HEADS
    kv_spec = lambda n: pl.BlockSpec((1, n, HEAD_DIM), lambda b, h, i: (b, 0, h))
    return pl.pallas_call(
        _attn_kernel,
        grid=(bn, N_KV_HEADS, s // ATTN_TQ),
        in_specs=[
            pl.BlockSpec((1, ATTN_TQ, gw), lambda b, h, i: (b, i, h)),
            kv_spec(n_ctx), kv_spec(n_lat), kv_spec(n_ctx), kv_spec(n_lat),
        ],
        out_specs=pl.BlockSpec((1, ATTN_TQ, gw), lambda b, h, i: (b, i, h)),
        out_shape=jax.ShapeDtypeStruct((bn, s, ATTN_W), BF16),
        scratch_shapes=[pltpu.VMEM((t, HEAD_DIM), BF16), pltpu.VMEM((t, 2 * HEAD_DIM), BF16)],
        compiler_params=_params(),
        name="gqa_attention",
    )(q, kc, kl, vc, vl)


def _residual_and_next(x, mix, gate, g_post, g_next, shift_next, scale_next):
    x1 = x + gate * _rms(mix, g_post)
    h = _rms(x1, g_next) * (1.0 + scale_next) + shift_next
    return x1, h.astype(BF16)


def _out_proj_kernel(a1_ref, a2_ref, w_ref, x_ref, gate_ref, g1_ref, g2_ref, sh_ref, sc_ref, x1_ref, h_ref):
    ka = a1_ref.shape[2]
    for r0 in range(0, x_ref.shape[1], EPI_SUB):
        rows = slice(r0, r0 + EPI_SUB)
        mix = jnp.dot(a1_ref[0, rows, :], w_ref[0:ka, :], preferred_element_type=F32)
        mix = mix + jnp.dot(a2_ref[0, rows, :], w_ref[ka:, :], preferred_element_type=F32)
        x1, h = _residual_and_next(x_ref[0, rows, :], mix, gate_ref[0], g1_ref[...], g2_ref[...],
                                   sh_ref[0], sc_ref[0])
        x1_ref[0, rows, :] = x1
        h_ref[0, rows, :] = h


def _out_proj(a1, a2, w, x, gate, g1, g2, shift, scale):
    bn, s, d = x.shape
    tm = OUT_TM
    row = lambda b, i: (b, i, 0)
    mod = lambda b, i: (b, 0, 0)
    vec = lambda b, i: (0, 0)
    return pl.pallas_call(
        _out_proj_kernel,
        grid=(bn, s // tm),
        in_specs=[
            pl.BlockSpec((1, tm, a1.shape[2]), row),
            pl.BlockSpec((1, tm, a2.shape[2]), row),
            pl.BlockSpec(w.shape, vec),
            pl.BlockSpec((1, tm, d), row),
            pl.BlockSpec((1, 1, d), mod),
            pl.BlockSpec((1, d), vec),
            pl.BlockSpec((1, d), vec),
            pl.BlockSpec((1, 1, d), mod),
            pl.BlockSpec((1, 1, d), mod),
        ],
        out_specs=[pl.BlockSpec((1, tm, d), row), pl.BlockSpec((1, tm, d), row)],
        out_shape=[jax.ShapeDtypeStruct((bn, s, d), F32), jax.ShapeDtypeStruct((bn, s, d), BF16)],
        compiler_params=_params(),
        name="out_proj_residual",
    )(a1, a2, w, x, gate, g1, g2, shift, scale)


def _ffn_kernel(*refs, has_next, na):
    if has_next:
        (h_ref, w1_ref, w2_ref, x_ref, gate_ref, g3_ref, gn_ref, sh_ref, sc_ref,
         x2_ref, hn_ref, t_scr) = refs
    else:
        h_ref, w1_ref, w2_ref, x_ref, gate_ref, g3_ref, x2_ref, t_scr = refs
    j = pl.program_id(2)
    ta = t_scr.shape[2]
    tb = w2_ref.shape[1]

    @pl.when(j < na)
    def _():
        t = jnp.dot(h_ref[0], w1_ref[...], preferred_element_type=F32)
        t = jnp.maximum(t, 0.0)
        t_scr[j] = (t * t).astype(BF16)

    @pl.when(j >= na)
    def _():
        y = jnp.dot(t_scr[0], w2_ref[0:ta, :], preferred_element_type=F32)
        for a in range(1, na):
            y = y + jnp.dot(t_scr[a], w2_ref[a * ta:(a + 1) * ta, :], preferred_element_type=F32)
        col = pl.multiple_of((j - na) * tb, tb)
        x2_ref[0, :, pl.ds(col, tb)] = y

    @pl.when(j == pl.num_programs(2) - 1)
    def _():
        if has_next:
            x2, hn = _residual_and_next(x_ref[0], x2_ref[0], gate_ref[0], g3_ref[...],
                                        gn_ref[...], sh_ref[0], sc_ref[0])
            x2_ref[0] = x2
            hn_ref[0] = hn
        else:
            x2_ref[0] = x_ref[0] + gate_ref[0] * _rms(x2_ref[0], g3_ref[...])


def _ffn(h, w1, w2, layer, x, gate, g3, nxt=None):
    bn, s, d = x.shape
    tm = FFN_TM
    na, ta = w1.shape[1], w1.shape[3]
    nb, tb = w2.shape[1], w2.shape[3]
    dff = na * ta
    row = lambda b, i, j: (b, i, 0)
    mod = lambda b, i, j: (b, 0, 0)
    vec = lambda b, i, j: (0, 0)
    in_specs = [
        pl.BlockSpec((1, tm, d), row),
        pl.BlockSpec((None, None, d, ta), lambda b, i, j: (layer, jnp.minimum(j, na - 1), 0, 0)),
        pl.BlockSpec((None, None, dff, tb), lambda b, i, j: (layer, jnp.maximum(j - na, 0), 0, 0)),
        pl.BlockSpec((1, tm, d), row),
        pl.BlockSpec((1, 1, d), mod),
        pl.BlockSpec((1, d), vec),
    ]
    args = [h, w1, w2, x, gate, g3]
    out_specs = [pl.BlockSpec((1, tm, d), row)]
    out_shape = [jax.ShapeDtypeStruct((bn, s, d), F32)]
    if nxt is not None:
        in_specs += [pl.BlockSpec((1, d), vec), pl.BlockSpec((1, 1, d), mod), pl.BlockSpec((1, 1, d), mod)]
        args += list(nxt)
        out_specs.append(pl.BlockSpec((1, tm, d), row))
        out_shape.append(jax.ShapeDtypeStruct((bn, s, d), BF16))
    return pl.pallas_call(
        functools.partial(_ffn_kernel, has_next=nxt is not None, na=na),
        grid=(bn, s // tm, na + nb),
        in_specs=in_specs,
        out_specs=out_specs,
        out_shape=out_shape,
        scratch_shapes=[pltpu.VMEM((na, tm, ta), BF16)],
        compiler_params=pltpu.CompilerParams(vmem_limit_bytes=FFN_VMEM_LIMIT_BYTES),
        name="sq_relu_mlp",
    )(*args)


def _gm_in_kernel(h_ref, w_ref, b_ref, z_ref):
    for c0 in range(0, w_ref.shape[1], GM_SUB):
        cols = slice(c0, c0 + GM_SUB)
        z = jnp.dot(h_ref[0], w_ref[:, cols], preferred_element_type=F32) + b_ref[:, cols]
        z_ref[0, :, cols] = _gelu(z)


def _gm_in(h, w, b):
    bn, s, d = h.shape
    n = w.shape[1]
    return pl.pallas_call(
        _gm_in_kernel,
        grid=(bn, s // GM_TM, n // GM_TN),
        in_specs=[
            pl.BlockSpec((1, GM_TM, d), lambda b_, i, j: (b_, i, 0)),
            pl.BlockSpec((d, GM_TN), lambda b_, i, j: (0, j)),
            pl.BlockSpec((1, GM_TN), lambda b_, i, j: (0, j)),
        ],
        out_specs=pl.BlockSpec((1, GM_TM, GM_TN), lambda b_, i, j: (b_, i, j)),
        out_shape=jax.ShapeDtypeStruct((bn, s, n), F32),
        compiler_params=_params(),
        name="gmlp_in_proj",
    )(h, w, b)


def _gm_out_kernel(u_ref, v_ref, vg_ref, vb_ref, wsp_ref, bsp_ref, w_ref, x_ref, gate_ref, g1_ref, g2_ref,
                   sh_ref, sc_ref, x1_ref, h_ref, vn_scr, gated_scr):
    for r0 in range(0, x_ref.shape[1], EPI_SUB):
        sub = slice(r0, r0 + EPI_SUB)
        v = v_ref[0, sub, :]
        mu = jnp.mean(v, axis=-1, keepdims=True)
        vc = v - mu
        var = jnp.mean(vc * vc, axis=-1, keepdims=True)
        vn_scr[sub, :] = (vc * lax.rsqrt(var + EPS) * vg_ref[...] + vb_ref[...]).astype(BF16)
        for c0 in range(r0, r0 + EPI_SUB, CHUNK):
            rows = slice(c0, c0 + CHUNK)
            for g in range(GM_GROUPS):
                cols = slice(g * GM_GROUP_W, (g + 1) * GM_GROUP_W)
                sv = jnp.dot(wsp_ref[g], vn_scr[rows, cols], preferred_element_type=F32) + bsp_ref[:, g:g + 1]
                gated_scr[rows, cols] = (u_ref[0, rows, cols] * sv).astype(BF16)
        mix = jnp.dot(gated_scr[sub, :], w_ref[...], preferred_element_type=F32)
        x1, h = _residual_and_next(x_ref[0, sub, :], mix, gate_ref[0], g1_ref[...], g2_ref[...],
                                   sh_ref[0], sc_ref[0])
        x1_ref[0, sub, :] = x1
        h_ref[0, sub, :] = h


def _gm_out(z, vg, vb, wsp, bsp_t, w, x, gate, g1, g2, shift, scale):
    bn, s, d = x.shape
    dg = z.shape[2] // 2
    tm = GMO_TM
    row = lambda b, i: (b, i, 0)
    mod = lambda b, i: (b, 0, 0)
    vec = lambda b, i: (0, 0)
    return pl.pallas_call(
        _gm_out_kernel,
        grid=(bn, s // tm),
        in_specs=[
            pl.BlockSpec((1, tm, dg), lambda b, i: (b, i, 0)),
            pl.BlockSpec((1, tm, dg), lambda b, i: (b, i, 1)),
            pl.BlockSpec((1, dg), vec),
            pl.BlockSpec((1, dg), vec),
            pl.BlockSpec(wsp.shape, lambda b, i: (0, 0, 0), pipeline_mode=pl.Buffered(1)),
            pl.BlockSpec(bsp_t.shape, vec),
            pl.BlockSpec(w.shape, vec, pipeline_mode=pl.Buffered(1)),
            pl.BlockSpec((1, tm, d), row),
            pl.BlockSpec((1, 1, d), mod),
            pl.BlockSpec((1, d), vec),
            pl.BlockSpec((1, d), vec),
            pl.BlockSpec((1, 1, d), mod),
            pl.BlockSpec((1, 1, d), mod),
        ],
        out_specs=[pl.BlockSpec((1, tm, d), row), pl.BlockSpec((1, tm, d), row)],
        out_shape=[jax.ShapeDtypeStruct((bn, s, d), F32), jax.ShapeDtypeStruct((bn, s, d), BF16)],
        scratch_shapes=[pltpu.VMEM((tm, dg), BF16), pltpu.VMEM((tm, dg), BF16)],
        compiler_params=_params(),
        name="gmlp_spatial_out_proj",
    )(z, z, vg, vb, wsp, bsp_t, w, x, gate, g1, g2, shift, scale)


def _rope_tables(n):
    t = jnp.arange(n)
    r_idx = (t // GRID_W).astype(F32)
    c_idx = (t % GRID_W).astype(F32)
    freqs = ROPE_THETA ** (-jnp.arange(ROPE_PAIRS, dtype=F32) / ROPE_PAIRS)
    ang_r = r_idx[:, None] * freqs
    ang_c = c_idx[:, None] * freqs
    cos = jnp.concatenate([jnp.cos(ang_r), jnp.cos(ang_r), jnp.cos(ang_c), jnp.cos(ang_c)], axis=-1)
    sin = jnp.concatenate([-jnp.sin(ang_r), jnp.sin(ang_r), -jnp.sin(ang_c), jnp.sin(ang_c)], axis=-1)
    return cos, sin


def kernel(x, c, ctx, c_ctx, w_mod, b_mod, norm_g, w_ff_in, w_ff_out, ar_w_in, ar_q_g, ar_k_g, ar_conv_w,
           ar_conv_b, ar_wa, ar_ba, ar_wx, ar_bx, ar_lambda, ar_w_out, gm_w_in, gm_b_in, gm_v_g, gm_v_b,
           gm_w_sp, gm_b_sp, gm_w_out):
    bn, s, d = x.shape
    depth = w_mod.shape[0]
    assert depth == 2, "layer pattern implemented for one attention/recurrent layer followed by one gMLP layer"

    rows = -(-(bn + 1) // SUBLANES) * SUBLANES
    cc = jnp.concatenate([c, c_ctx[None, :], jnp.zeros((rows - bn - 1, d), F32)], axis=0)
    mods = _modulation(cc, w_mod, b_mod)

    def lat_mod(layer, k):
        return mods[layer, :bn, k * d:(k + 1) * d].reshape(bn, 1, d)

    def ctx_mod(layer, k):
        return mods[layer, bn:bn + 1, k * d:(k + 1) * d].reshape(1, 1, d)

    g = norm_g.reshape(depth, 4, 1, d)
    w_ff1 = _to_bf16_col_blocks(w_ff_in, FFN_TA)
    w_ff2 = _to_bf16_col_blocks(w_ff_out, FFN_TB)

    w_in = ar_w_in[0].astype(BF16)
    qg = ar_q_g[0].reshape(1, HEAD_DIM)
    kg = ar_k_g[0].reshape(1, HEAD_DIM)
    cos, sin = _rope_tables(s)
    ql, kl, vl, xrl, grl = _in_proj(x, lat_mod(0, 0), lat_mod(0, 1), g[0, 0], w_in, qg, kg, cos, sin, latent=True)
    kc, vc, xrc = _in_proj(ctx, ctx_mod(0, 0), ctx_mod(0, 1), g[0, 0], w_in, qg, kg, None, None, latent=False)

    attn = _attention(ql, kc, kl, vc, vl)

    w_gate = jnp.concatenate([ar_wa[0], ar_wx[0]], axis=-1).astype(BF16)
    b_gate = jnp.concatenate([ar_ba[0].reshape(2, RNN_BLOCKS, 1, RNN_BLOCK_W),
                              ar_bx[0].reshape(2, RNN_BLOCKS, 1, RNN_BLOCK_W)], axis=-1)
    lam = ar_lambda[0].reshape(2, RNN_BLOCKS, 1, RNN_BLOCK_W)
    rnn = _rglru(xrl, xrc, grl, ar_conv_w[0], ar_conv_b[0].reshape(1, D_RNN), w_gate, b_gate, lam)

    x1, h = _out_proj(attn, rnn, ar_w_out[0].astype(BF16), x, lat_mod(0, 2), g[0, 1], g[0, 2],
                      lat_mod(0, 3), lat_mod(0, 4))
    x2, h = _ffn(h, w_ff1, w_ff2, 0, x1, lat_mod(0, 5), g[0, 3], nxt=(g[1, 0], lat_mod(1, 0), lat_mod(1, 1)))

    z = _gm_in(h, gm_w_in[0].astype(BF16), gm_b_in[0].reshape(1, -1))
    x3, h = _gm_out(z, gm_v_g[0].reshape(1, -1), gm_v_b[0].reshape(1, -1), gm_w_sp[0].astype(BF16),
                    gm_b_sp[0].T, gm_w_out[0].astype(BF16), x2, lat_mod(1, 2), g[1, 1], g[1, 2],
                    lat_mod(1, 3), lat_mod(1, 4))
    (x4,) = _ffn(h, w_ff1, w_ff2, 1, x3, lat_mod(1, 5), g[1, 3])
    return x4
```

```python
import functools

import jax
import jax.numpy as jnp
from jax import lax
from jax.experimental import pallas as pl
from jax.experimental.pallas import tpu as pltpu

F32 = jnp.float32
BF16 = jnp.bfloat16

GRID_W = 64
N_HEADS = 8
N_KV_HEADS = 2
HEAD_DIM = 128
ATTN_W = N_HEADS * HEAD_DIM
KV_W = N_KV_HEADS * HEAD_DIM
ROPE_THETA = 10000.0
ROPE_PAIRS = HEAD_DIM // 4
D_RNN = 1024
RNN_BLOCKS = 8
RNN_BLOCK_W = D_RNN // RNN_BLOCKS
CONV_W = 4
RG_C = 8.0
GM_GROUPS = 16
GM_GROUP_W = 128
CHUNK = 128
EPS = 1e-6

LANES = 128
SUBLANES = 8
VMEM_LIMIT_BYTES = 56 * 1024 * 1024
FFN_VMEM_LIMIT_BYTES = 63 * 1024 * 1024

MOD_TN = 1024
PROJ_TN = 512
PROJ_TM = 512
SCAN_TC = 256
SCAN_TJ = 32
SEG_PAD = 8
ATTN_TQ = 1024
ATTN_SUB = 256
OUT_TM = 512
EPI_SUB = 256
FFN_TM = 512
FFN_TA = 1024
FFN_TB = 512
GM_TM = 1024
GM_TN = 2048
GM_SUB = 1024
GMO_TM = 512
CAST_BLOCK_BYTES = 8 * 1024 * 1024


def _params():
    return pltpu.CompilerParams(vmem_limit_bytes=VMEM_LIMIT_BYTES)


def _rms(x, g):
    ms = jnp.mean(x * x, axis=-1, keepdims=True)
    return x * lax.rsqrt(ms + EPS) * g


def _sigmoid(x):
    return 1.0 / (1.0 + jnp.exp(-x))


def _gelu(x):
    c = 0.7978845608028654
    return 0.5 * x * (1.0 + jnp.tanh(c * (x + 0.044715 * (x * x * x))))


def _mod_kernel(cc_ref, w_ref, b_ref, o_ref):
    c = cc_ref[...]
    s = c * _sigmoid(c)
    o_ref[0] = jnp.dot(s.astype(BF16), w_ref[0].astype(BF16), preferred_element_type=F32) + b_ref[0]


def _modulation(cc, w_mod, b_mod):
    depth, d, n = w_mod.shape
    rows = cc.shape[0]
    return pl.pallas_call(
        _mod_kernel,
        grid=(depth, n // MOD_TN),
        in_specs=[
            pl.BlockSpec((rows, d), lambda l, j: (0, 0)),
            pl.BlockSpec((1, d, MOD_TN), lambda l, j: (l, 0, j)),
            pl.BlockSpec((1, 1, MOD_TN), lambda l, j: (l, 0, j)),
        ],
        out_specs=pl.BlockSpec((1, rows, MOD_TN), lambda l, j: (l, 0, j)),
        out_shape=jax.ShapeDtypeStruct((depth, rows, n), F32),
        compiler_params=_params(),
        name="adaln_modulation",
    )(cc, w_mod, b_mod.reshape(depth, 1, n))


def _cast_kernel(w_ref, o_ref):
    tc = o_ref.shape[3]
    for a in range(o_ref.shape[1]):
        o_ref[0, a] = w_ref[0, :, a * tc:(a + 1) * tc].astype(BF16)


def _to_bf16_col_blocks(w, tc):
    layers, r, c = w.shape
    tr = min(r, CAST_BLOCK_BYTES // (c * 4))
    return pl.pallas_call(
        _cast_kernel,
        grid=(layers, r // tr),
        in_specs=[pl.BlockSpec((1, tr, c), lambda l, i: (l, i, 0))],
        out_specs=pl.BlockSpec((1, c // tc, tr, tc), lambda l, i: (l, 0, i, 0)),
        out_shape=jax.ShapeDtypeStruct((layers, c // tc, r, tc), BF16),
        compiler_params=_params(),
        name="weights_to_bf16",
    )(w)


def _head_norm_rope(xh, gain, cos, sin):
    y = _rms(xh, gain)
    if cos is None:
        return y
    lane = lax.broadcasted_iota(jnp.int32, y.shape, 1)
    first_half = (lane % (2 * ROPE_PAIRS)) < ROPE_PAIRS
    partner = jnp.where(first_half,
                        pltpu.roll(y, HEAD_DIM - ROPE_PAIRS, 1),
                        pltpu.roll(y, ROPE_PAIRS, 1))
    return y * cos + partner * sin


def _in_proj_kernel(*refs, latent):
    if latent:
        (x_ref, sh_ref, sc_ref, g_ref, w_ref, qg_ref, kg_ref, cos_ref, sin_ref,
         q_ref, k_ref, v_ref, xr_ref, gr_ref, h_scr) = refs
        cos, sin = cos_ref[...], sin_ref[...]
    else:
        (x_ref, sh_ref, sc_ref, g_ref, w_ref, kg_ref, k_ref, v_ref, xr_ref, h_scr) = refs
        cos = sin = None
    h = _rms(x_ref[0], g_ref[...] * (1.0 + sc_ref[0])) + sh_ref[0]
    h_scr[...] = h.astype(BF16)

    def project(c0, width):
        return jnp.dot(h_scr[...], w_ref[:, c0:c0 + width], preferred_element_type=F32)

    if latent:
        for c0 in range(0, ATTN_W, PROJ_TN):
            acc = project(c0, PROJ_TN)
            for hh in range(PROJ_TN // HEAD_DIM):
                sl = slice(hh * HEAD_DIM, (hh + 1) * HEAD_DIM)
                dst = slice(c0 + hh * HEAD_DIM, c0 + (hh + 1) * HEAD_DIM)
                q_ref[0, :, dst] = _head_norm_rope(acc[:, sl], qg_ref[...], cos, sin).astype(BF16)

    acc = project(ATTN_W, 2 * KV_W)
    for hh in range(N_KV_HEADS):
        sl = slice(hh * HEAD_DIM, (hh + 1) * HEAD_DIM)
        k_ref[0, :, sl] = _head_norm_rope(acc[:, sl], kg_ref[...], cos, sin).astype(BF16)
    v_ref[0] = acc[:, KV_W:2 * KV_W].astype(BF16)

    rnn0 = ATTN_W + 2 * KV_W
    for c0 in range(0, D_RNN, PROJ_TN):
        xr_ref[0, :, c0:c0 + PROJ_TN] = project(rnn0 + c0, PROJ_TN)
    if latent:
        for c0 in range(0, D_RNN, PROJ_TN):
            gr_ref[0, :, c0:c0 + PROJ_TN] = project(rnn0 + D_RNN + c0, PROJ_TN)


def _in_proj(x, shift, scale, g, w, qg, kg, cos, sin, *, latent):
    bn, length, d = x.shape
    tm = min(PROJ_TM, length)
    per_batch = shift.shape[0] > 1
    mod_map = (lambda b, i: (b, 0, 0)) if per_batch else (lambda b, i: (0, 0, 0))
    vec_map = lambda b, i: (0, 0)
    row = lambda b, i: (b, i, 0)

    in_specs = [
        pl.BlockSpec((1, tm, d), row),
        pl.BlockSpec((1, 1, d), mod_map),
        pl.BlockSpec((1, 1, d), mod_map),
        pl.BlockSpec((1, d), vec_map),
        pl.BlockSpec(w.shape, vec_map),
    ]
    args = [x, shift, scale, g, w]
    k_spec = pl.BlockSpec((1, tm, KV_W), row)
    xr_spec = pl.BlockSpec((1, tm, D_RNN), row)
    k_shape = jax.ShapeDtypeStruct((bn, length, KV_W), BF16)
    xr_shape = jax.ShapeDtypeStruct((bn, length, D_RNN), F32)
    if latent:
        in_specs += [pl.BlockSpec((1, HEAD_DIM), vec_map), pl.BlockSpec((1, HEAD_DIM), vec_map),
                     pl.BlockSpec((tm, HEAD_DIM), lambda b, i: (i, 0)),
                     pl.BlockSpec((tm, HEAD_DIM), lambda b, i: (i, 0))]
        args += [qg, kg, cos, sin]
        out_specs = [pl.BlockSpec((1, tm, ATTN_W), row), k_spec, k_spec, xr_spec, xr_spec]
        out_shape = [jax.ShapeDtypeStruct((bn, length, ATTN_W), BF16), k_shape, k_shape, xr_shape, xr_shape]
    else:
        in_specs += [pl.BlockSpec((1, HEAD_DIM), vec_map)]
        args += [kg]
        out_specs = [k_spec, k_spec, xr_spec]
        out_shape = [k_shape, k_shape, xr_shape]

    return pl.pallas_call(
        functools.partial(_in_proj_kernel, latent=latent),
        grid=(bn, length // tm),
        in_specs=in_specs,
        out_specs=out_specs,
        out_shape=out_shape,
        scratch_shapes=[pltpu.VMEM((tm, d), BF16)],
        compiler_params=_params(),
        name="in_proj_latent" if latent else "in_proj_context",
    )(*args)


def _dwconv(x, w, b):
    n = x.shape[0]
    row = lax.broadcasted_iota(jnp.int32, x.shape, 0)
    y = b + x * w[2:3]
    y = y + jnp.where(row >= 2, pltpu.roll(x, 2, 0), 0.0) * w[0:1]
    y = y + jnp.where(row >= 1, pltpu.roll(x, 1, 0), 0.0) * w[1:2]
    y = y + jnp.where(row < n - 1, pltpu.roll(x, n - 1, 0), 0.0) * w[3:4]
    return y


def _scan_chunk(a, b, carry, reverse):
    tc = a.shape[0]
    nv = tc // SUBLANES
    a3 = a.reshape(nv, SUBLANES, LANES)
    b3 = b.reshape(nv, SUBLANES, LANES)
    row = lax.broadcasted_iota(jnp.int32, a3.shape, 1)
    step = 1
    while step < SUBLANES:
        if reverse:
            shift, valid = SUBLANES - step, row < SUBLANES - step
        else:
            shift, valid = step, row >= step
        a_sh = pltpu.roll(a3, shift, 1)
        b_sh = pltpu.roll(b3, shift, 1)
        b3 = jnp.where(valid, a3 * b_sh + b3, b3)
        a3 = jnp.where(valid, a3 * a_sh, a3)
        step *= 2
    last = 0 if reverse else SUBLANES - 1
    a_last = jnp.broadcast_to(a3[:, last:last + 1, :], a3.shape)
    b_last = jnp.broadcast_to(b3[:, last:last + 1, :], b3.shape)
    hs = [None] * nv
    for v in (range(nv - 1, -1, -1) if reverse else range(nv)):
        hs[v] = b3[v] + a3[v] * carry
        carry = b_last[v] + a_last[v] * carry
    return hs, carry


def _rglru_kernel(xl_ref, xc_ref, gl_ref, cw_ref, cb_ref, w_ref, bias_ref, lam_ref, o_ref,
                  xpad, xconv_c, xcs, hloc_f, aloc_f, hloc_b, aloc_b, hsum):
    cw = cw_ref[...]
    cb = cb_ref[...]
    length = xl_ref.shape[1]
    tseg = length // SUBLANES
    pitch = tseg + SEG_PAD
    n_j = tseg // SCAN_TJ
    n_ctx = xc_ref.shape[1] // SCAN_TC

    xconv_c[...] = _dwconv(xc_ref[0], cw, cb)

    pad_zeros = jnp.zeros((SEG_PAD, LANES), F32)
    xpad[0:SEG_PAD, :] = pad_zeros
    for s in range(SUBLANES):
        base = SEG_PAD + s * pitch
        xpad[base:base + tseg, :] = xl_ref[0, s * tseg:(s + 1) * tseg, :]
        pad = base + tseg
        xpad[pad:pad + SEG_PAD, :] = pad_zeros
        if s + 1 < SUBLANES:
            xpad[pad:pad + 1, :] = xl_ref[0, (s + 1) * tseg:(s + 1) * tseg + 1, :]
        xpad[pad + SEG_PAD - 2:pad + SEG_PAD, :] = xl_ref[0, (s + 1) * tseg - 2:(s + 1) * tseg, :]

    taps = [jnp.broadcast_to(cw[t:t + 1], (SUBLANES, LANES)) for t in range(CONV_W)]
    bias_rows = jnp.broadcast_to(cb, (SUBLANES, LANES))

    def conv_step(kk, _):
        j0 = kk * SCAN_TJ
        xs = [xpad[pl.ds(SEG_PAD + j0 - CONV_W // 2 + t, SUBLANES, stride=pitch), :]
              for t in range(SCAN_TJ + CONV_W - 1)]
        for jj in range(SCAN_TJ):
            y = bias_rows
            for t in range(CONV_W):
                y = y + xs[jj + t] * taps[t]
            xcs[pl.ds(pl.multiple_of((j0 + jj) * SUBLANES, SUBLANES), SUBLANES), :] = y
        return 0

    lax.fori_loop(0, n_j, conv_step, 0)

    def gates(xc, d):
        z = jnp.dot(xc.astype(BF16), w_ref[d, 0], preferred_element_type=F32) + bias_ref[d, 0]
        t_r = jnp.tanh(0.5 * z[:, :RNN_BLOCK_W])
        i = 0.5 * jnp.tanh(0.5 * z[:, RNN_BLOCK_W:]) + 0.5
        lam = lam_ref[d, 0]
        softplus_neg_lam = jnp.maximum(-lam, 0.0) + jnp.log1p(jnp.exp(-jnp.abs(lam)))
        half_rate = (-0.5 * RG_C) * softplus_neg_lam
        log_a = half_rate * t_r + half_rate
        a = jnp.exp(log_a)
        th = jnp.tanh(log_a)
        b = jnp.sqrt(-2.0 * th / (1.0 - th)) * (i * xc)
        return a, b

    ctx_state = []
    for d, reverse in ((0, False), (1, True)):
        def ctx_step(kk, carry, d=d, reverse=reverse):
            k = (n_ctx - 1 - kk) if reverse else kk
            r0 = pl.multiple_of(k * SCAN_TC, SCAN_TC)
            a, b = gates(xconv_c[pl.ds(r0, SCAN_TC), :], d)
            _, carry = _scan_chunk(a, b, carry, reverse)
            return carry

        ctx_state.append(lax.fori_loop(0, n_ctx, ctx_step, jnp.zeros((SUBLANES, LANES), F32)))

    def local_step(kk, carry):
        new = []
        for d, (h, acc_a) in enumerate(carry):
            reverse = d == 1
            j0 = ((n_j - 1 - kk) if reverse else kk) * SCAN_TJ
            src = pl.ds(pl.multiple_of(j0 * SUBLANES, SCAN_TJ * SUBLANES), SCAN_TJ * SUBLANES)
            a, b = gates(xcs[src, :], d)
            hloc, aloc = (hloc_b, aloc_b) if reverse else (hloc_f, aloc_f)
            for jj in (range(SCAN_TJ - 1, -1, -1) if reverse else range(SCAN_TJ)):
                rows = slice(jj * SUBLANES, (jj + 1) * SUBLANES)
                h = a[rows] * h + b[rows]
                acc_a = a[rows] * acc_a
                dst = pl.ds(pl.multiple_of((j0 + jj) * SUBLANES, SUBLANES), SUBLANES)
                hloc[dst, :] = h
                aloc[dst, :] = acc_a
            new.append((h, acc_a))
        return tuple(new)

    zeros = jnp.zeros((SUBLANES, LANES), F32)
    ones = jnp.ones((SUBLANES, LANES), F32)
    (h_f, a_f), (h_b, a_b) = lax.fori_loop(0, n_j, local_step, ((zeros, ones), (zeros, ones)))

    def entry_states(h_end, a_end, c_in, reverse):
        row = lax.broadcasted_iota(jnp.int32, (SUBLANES, LANES), 0)
        first = SUBLANES - 1 if reverse else 0
        c = c_in[0:1]
        out = jnp.where(row == first, jnp.broadcast_to(c, (SUBLANES, LANES)), 0.0)
        for k in range(1, SUBLANES):
            s = first - k if reverse else first + k
            prev = s + 1 if reverse else s - 1
            c = h_end[prev:prev + 1] + a_end[prev:prev + 1] * c
            out = jnp.where(row == s, jnp.broadcast_to(c, (SUBLANES, LANES)), out)
        return out

    c_f = entry_states(h_f, a_f, ctx_state[0], False)
    c_b = entry_states(h_b, a_b, ctx_state[1], True)

    def fix_step(kk, _):
        j0 = kk * SCAN_TJ
        for jj in range(SCAN_TJ):
            src = pl.ds(pl.multiple_of((j0 + jj) * SUBLANES, SUBLANES), SUBLANES)
            h = (hloc_f[src, :] + aloc_f[src, :] * c_f) + (hloc_b[src, :] + aloc_b[src, :] * c_b)
            hsum[pl.ds(j0 + jj, SUBLANES, stride=pitch), :] = h
        return 0

    lax.fori_loop(0, n_j, fix_step, 0)

    for s in range(SUBLANES):
        for r0 in range(0, tseg, SCAN_TC):
            src = slice(s * pitch + r0, s * pitch + r0 + SCAN_TC)
            dst = slice(s * tseg + r0, s * tseg + r0 + SCAN_TC)
            o_ref[0, dst, :] = (hsum[src, :] * _gelu(gl_ref[0, dst, :])).astype(BF16)


def _rglru(xl, xc, gl, conv_w, conv_b, w_gate, b_gate, lam):
    bn, s, _ = xl.shape
    ctx_len = xc.shape[1]
    bw = RNN_BLOCK_W
    padded = s + SUBLANES * SEG_PAD
    col = lambda b, n: (b, 0, n)
    return pl.pallas_call(
        _rglru_kernel,
        grid=(bn, RNN_BLOCKS),
        in_specs=[
            pl.BlockSpec((1, s, bw), col),
            pl.BlockSpec((1, ctx_len, bw), col),
            pl.BlockSpec((1, s, bw), col),
            pl.BlockSpec((CONV_W, bw), lambda b, n: (0, n)),
            pl.BlockSpec((1, bw), lambda b, n: (0, n)),
            pl.BlockSpec((2, 1, bw, 2 * bw), lambda b, n: (0, n, 0, 0)),
            pl.BlockSpec((2, 1, 1, 2 * bw), lambda b, n: (0, n, 0, 0)),
            pl.BlockSpec((2, 1, 1, bw), lambda b, n: (0, n, 0, 0)),
        ],
        out_specs=pl.BlockSpec((1, s, bw), col),
        out_shape=jax.ShapeDtypeStruct((bn, s, D_RNN), BF16),
        scratch_shapes=[
            pltpu.VMEM((SEG_PAD + padded, bw), F32),
            pltpu.VMEM((ctx_len, bw), F32),
            pltpu.VMEM((s, bw), F32),
            pltpu.VMEM((s, bw), F32), pltpu.VMEM((s, bw), F32),
            pltpu.VMEM((s, bw), F32), pltpu.VMEM((s, bw), F32),
            pltpu.VMEM((padded, bw), F32),
        ],
        compiler_params=_params(),
        name="rglru_bidirectional",
    )(xl, xc, gl, conv_w, conv_b, w_gate, b_gate, lam)


def _attn_kernel(q_ref, kc_ref, kl_ref, vc_ref, vl_ref, o_ref, k_scr, v_scr):
    n_ctx = kc_ref.shape[1]

    @pl.when(pl.program_id(2) == 0)
    def _():
        k_scr[0:n_ctx, :] = kc_ref[0]
        k_scr[n_ctx:, :] = kl_ref[0]
        v_scr[0:n_ctx, 0:HEAD_DIM] = vc_ref[0]
        v_scr[n_ctx:, 0:HEAD_DIM] = vl_ref[0]
        v_scr[:, HEAD_DIM:] = jnp.ones((v_scr.shape[0], HEAD_DIM), BF16)

    k = k_scr[...]
    v = v_scr[...]
    c = (HEAD_DIM ** -0.5) * 1.4426950408889634
    for r0 in range(0, q_ref.shape[1], ATTN_SUB):
        rows = slice(r0, r0 + ATTN_SUB)
        for hh in range(N_HEADS // N_KV_HEADS):
            sl = slice(hh * HEAD_DIM, (hh + 1) * HEAD_DIM)
            s = lax.dot_general(q_ref[0, rows, sl], k, (((1,), (1,)), ((), ())), preferred_element_type=F32)
            m = jnp.max(s, axis=-1, keepdims=True)
            p = jnp.exp2((s - m) * c).astype(BF16)
            ov = jnp.dot(p, v, preferred_element_type=F32)
            o_ref[0, rows, sl] = (ov[:, :HEAD_DIM] / ov[:, HEAD_DIM:HEAD_DIM + 1]).astype(BF16)


def _attention(q, kc, kl, vc, vl):
    bn, s, _ = q.shape
    n_ctx, n_lat = kc.shape[1], kl.shape[1]
    t = n_ctx + n_lat
    gw = ATTN_W // N_KV_HEADS
    kv_spec = lambda n: pl.BlockSpec((1, n, HEAD_DIM), lambda b, h, i: (b, 0, h))
    return pl.pallas_call(
        _attn_kernel,
        grid=(bn, N_KV_HEADS, s // ATTN_TQ),
        in_specs=[
            pl.BlockSpec((1, ATTN_TQ, gw), lambda b, h, i: (b, i, h)),
            kv_spec(n_ctx), kv_spec(n_lat), kv_spec(n_ctx), kv_spec(n_lat),
        ],
        out_specs=pl.BlockSpec((1, ATTN_TQ, gw), lambda b, h, i: (b, i, h)),
        out_shape=jax.ShapeDtypeStruct((bn, s, ATTN_W), BF16),
        scratch_shapes=[pltpu.VMEM((t, HEAD_DIM), BF16), pltpu.VMEM((t, 2 * HEAD_DIM), BF16)],
        compiler_params=_params(),
        name="gqa_attention",
    )(q, kc, kl, vc, vl)


def _residual_and_next(x, mix, gate, g_post, g_next, shift_next, scale_next):
    x1 = x + _rms(mix, gate * g_post)
    h = _rms(x1, g_next * (1.0 + scale_next)) + shift_next
    return x1, h.astype(BF16)


def _out_proj_kernel(a1_ref, a2_ref, w_ref, x_ref, gate_ref, g1_ref, g2_ref, sh_ref, sc_ref, x1_ref, h_ref):
    ka = a1_ref.shape[2]
    for r0 in range(0, x_ref.shape[1], EPI_SUB):
        rows = slice(r0, r0 + EPI_SUB)
        mix = jnp.dot(a1_ref[0, rows, :], w_ref[0:ka, :], preferred_element_type=F32)
        mix = mix + jnp.dot(a2_ref[0, rows, :], w_ref[ka:, :], preferred_element_type=F32)
        x1, h = _residual_and_next(x_ref[0, rows, :], mix, gate_ref[0], g1_ref[...], g2_ref[...],
                                   sh_ref[0], sc_ref[0])
        x1_ref[0, rows, :] = x1
        h_ref[0, rows, :] = h


def _out_proj(a1, a2, w, x, gate, g1, g2, shift, scale):
    bn, s, d = x.shape
    tm = OUT_TM
    row = lambda b, i: (b, i, 0)
    mod = lambda b, i: (b, 0, 0)
    vec = lambda b, i: (0, 0)
    return pl.pallas_call(
        _out_proj_kernel,
        grid=(bn, s // tm),
        in_specs=[
            pl.BlockSpec((1, tm, a1.shape[2]), row),
            pl.BlockSpec((1, tm, a2.shape[2]), row),
            pl.BlockSpec(w.shape, vec),
            pl.BlockSpec((1, tm, d), row),
            pl.BlockSpec((1, 1, d), mod),
            pl.BlockSpec((1, d), vec),
            pl.BlockSpec((1, d), vec),
            pl.BlockSpec((1, 1, d), mod),
            pl.BlockSpec((1, 1, d), mod),
        ],
        out_specs=[pl.BlockSpec((1, tm, d), row), pl.BlockSpec((1, tm, d), row)],
        out_shape=[jax.ShapeDtypeStruct((bn, s, d), F32), jax.ShapeDtypeStruct((bn, s, d), BF16)],
        compiler_params=_params(),
        name="out_proj_residual",
    )(a1, a2, w, x, gate, g1, g2, shift, scale)


def _ffn_kernel(*refs, has_next, na):
    if has_next:
        (h_ref, w1_ref, w2_ref, x_ref, gate_ref, g3_ref, gn_ref, sh_ref, sc_ref,
         x2_ref, hn_ref, t_scr) = refs
    else:
        h_ref, w1_ref, w2_ref, x_ref, gate_ref, g3_ref, x2_ref, t_scr = refs
    j = pl.program_id(2)
    ta = t_scr.shape[2]
    tb = w2_ref.shape[1]

    @pl.when(j < na)
    def _():
        t = jnp.dot(h_ref[0], w1_ref[...], preferred_element_type=F32)
        t = jnp.maximum(t, 0.0)
        t_scr[j] = (t * t).astype(BF16)

    last = pl.num_programs(2) - 1

    def down_proj(rows):
        y = jnp.dot(t_scr[0, rows, :], w2_ref[0:ta, :], preferred_element_type=F32)
        for a in range(1, na):
            y = y + jnp.dot(t_scr[a, rows, :], w2_ref[a * ta:(a + 1) * ta, :], preferred_element_type=F32)
        return y

    @pl.when(jnp.logical_and(j >= na, j < last))
    def _():
        col = pl.multiple_of((j - na) * tb, tb)
        x2_ref[0, :, pl.ds(col, tb)] = down_proj(slice(None))

    @pl.when(j == last)
    def _():
        col0 = x2_ref.shape[2] - tb
        for r0 in range(0, x2_ref.shape[1], EPI_SUB):
            rows = slice(r0, r0 + EPI_SUB)
            x2_ref[0, rows, col0:] = down_proj(rows)
            if has_next:
                x2, hn = _residual_and_next(x_ref[0, rows, :], x2_ref[0, rows, :], gate_ref[0], g3_ref[...],
                                            gn_ref[...], sh_ref[0], sc_ref[0])
                x2_ref[0, rows, :] = x2
                hn_ref[0, rows, :] = hn
            else:
                x2_ref[0, rows, :] = x_ref[0, rows, :] + _rms(x2_ref[0, rows, :], gate_ref[0] * g3_ref[...])


def _ffn(h, w1, w2, layer, x, gate, g3, nxt=None):
    bn, s, d = x.shape
    tm = FFN_TM
    na, ta = w1.shape[1], w1.shape[3]
    nb, tb = w2.shape[1], w2.shape[3]
    dff = na * ta
    row = lambda b, i, j: (b, i, 0)
    mod = lambda b, i, j: (b, 0, 0)
    vec = lambda b, i, j: (0, 0)
    in_specs = [
        pl.BlockSpec((1, tm, d), row),
        pl.BlockSpec((None, None, d, ta), lambda b, i, j: (layer, jnp.minimum(j, na - 1), 0, 0)),
        pl.BlockSpec((None, None, dff, tb), lambda b, i, j: (layer, jnp.maximum(j - na, 0), 0, 0)),
        pl.BlockSpec((1, tm, d), row),
        pl.BlockSpec((1, 1, d), mod),
        pl.BlockSpec((1, d), vec),
    ]
    args = [h, w1, w2, x, gate, g3]
    out_specs = [pl.BlockSpec((1, tm, d), row)]
    out_shape = [jax.ShapeDtypeStruct((bn, s, d), F32)]
    if nxt is not None:
        in_specs += [pl.BlockSpec((1, d), vec), pl.BlockSpec((1, 1, d), mod), pl.BlockSpec((1, 1, d), mod)]
        args += list(nxt)
        out_specs.append(pl.BlockSpec((1, tm, d), row))
        out_shape.append(jax.ShapeDtypeStruct((bn, s, d), BF16))
    return pl.pallas_call(
        functools.partial(_ffn_kernel, has_next=nxt is not None, na=na),
        grid=(bn, s // tm, na + nb),
        in_specs=in_specs,
        out_specs=out_specs,
        out_shape=out_shape,
        scratch_shapes=[pltpu.VMEM((na, tm, ta), BF16)],
        compiler_params=pltpu.CompilerParams(vmem_limit_bytes=FFN_VMEM_LIMIT_BYTES),
        name="sq_relu_mlp",
    )(*args)


def _gm_in_kernel(h_ref, w_ref, b_ref, z_ref):
    for c0 in range(0, w_ref.shape[1], GM_SUB):
        cols = slice(c0, c0 + GM_SUB)
        z = jnp.dot(h_ref[0], w_ref[:, cols], preferred_element_type=F32) + b_ref[:, cols]
        z_ref[0, :, cols] = _gelu(z)


def _gm_in(h, w, b):
    bn, s, d = h.shape
    n = w.shape[1]
    return pl.pallas_call(
        _gm_in_kernel,
        grid=(bn, s // GM_TM, n // GM_TN),
        in_specs=[
            pl.BlockSpec((1, GM_TM, d), lambda b_, i, j: (b_, i, 0)),
            pl.BlockSpec((d, GM_TN), lambda b_, i, j: (0, j)),
            pl.BlockSpec((1, GM_TN), lambda b_, i, j: (0, j)),
        ],
        out_specs=pl.BlockSpec((1, GM_TM, GM_TN), lambda b_, i, j: (b_, i, j)),
        out_shape=jax.ShapeDtypeStruct((bn, s, n), F32),
        compiler_params=_params(),
        name="gmlp_in_proj",
    )(h, w, b)


def _gm_out_kernel(u_ref, v_ref, vg_ref, vb_ref, wsp_ref, bsp_ref, w_ref, x_ref, gate_ref, g1_ref, g2_ref,
                   sh_ref, sc_ref, x1_ref, h_ref, vn_scr, gated_scr):
    for r0 in range(0, x_ref.shape[1], EPI_SUB):
        sub = slice(r0, r0 + EPI_SUB)
        v = v_ref[0, sub, :]
        mu = jnp.mean(v, axis=-1, keepdims=True)
        vc = v - mu
        var = jnp.mean(vc * vc, axis=-1, keepdims=True)
        vn_scr[sub, :] = (vc * lax.rsqrt(var + EPS) * vg_ref[...] + vb_ref[...]).astype(BF16)
        for c0 in range(r0, r0 + EPI_SUB, CHUNK):
            rows = slice(c0, c0 + CHUNK)
            for g in range(GM_GROUPS):
                cols = slice(g * GM_GROUP_W, (g + 1) * GM_GROUP_W)
                sv = jnp.dot(wsp_ref[g], vn_scr[rows, cols], preferred_element_type=F32) + bsp_ref[:, g:g + 1]
                gated_scr[rows, cols] = (u_ref[0, rows, cols] * sv).astype(BF16)
        mix = jnp.dot(gated_scr[sub, :], w_ref[...], preferred_element_type=F32)
        x1, h = _residual_and_next(x_ref[0, sub, :], mix, gate_ref[0], g1_ref[...], g2_ref[...],
                                   sh_ref[0], sc_ref[0])
        x1_ref[0, sub, :] = x1
        h_ref[0, sub, :] = h


def _gm_out(z, vg, vb, wsp, bsp_t, w, x, gate, g1, g2, shift, scale):
    bn, s, d = x.shape
    dg = z.shape[2] // 2
    tm = GMO_TM
    row = lambda b, i: (b, i, 0)
    mod = lambda b, i: (b, 0, 0)
    vec = lambda b, i: (0, 0)
    return pl.pallas_call(
        _gm_out_kernel,
        grid=(bn, s // tm),
        in_specs=[
            pl.BlockSpec((1, tm, dg), lambda b, i: (b, i, 0)),
            pl.BlockSpec((1, tm, dg), lambda b, i: (b, i, 1)),
            pl.BlockSpec((1, dg), vec),
            pl.BlockSpec((1, dg), vec),
            pl.BlockSpec(wsp.shape, lambda b, i: (0, 0, 0), pipeline_mode=pl.Buffered(1)),
            pl.BlockSpec(bsp_t.shape, vec),
            pl.BlockSpec(w.shape, vec, pipeline_mode=pl.Buffered(1)),
            pl.BlockSpec((1, tm, d), row),
            pl.BlockSpec((1, 1, d), mod),
            pl.BlockSpec((1, d), vec),
            pl.BlockSpec((1, d), vec),
            pl.BlockSpec((1, 1, d), mod),
            pl.BlockSpec((1, 1, d), mod),
        ],
        out_specs=[pl.BlockSpec((1, tm, d), row), pl.BlockSpec((1, tm, d), row)],
        out_shape=[jax.ShapeDtypeStruct((bn, s, d), F32), jax.ShapeDtypeStruct((bn, s, d), BF16)],
        scratch_shapes=[pltpu.VMEM((tm, dg), BF16), pltpu.VMEM((tm, dg), BF16)],
        compiler_params=_params(),
        name="gmlp_spatial_out_proj",
    )(z, z, vg, vb, wsp, bsp_t, w, x, gate, g1, g2, shift, scale)


def _rope_tables(n):
    t = jnp.arange(n)
    r_idx = (t // GRID_W).astype(F32)
    c_idx = (t % GRID_W).astype(F32)
    freqs = ROPE_THETA ** (-jnp.arange(ROPE_PAIRS, dtype=F32) / ROPE_PAIRS)
    ang_r = r_idx[:, None] * freqs
    ang_c = c_idx[:, None] * freqs
    cos = jnp.concatenate([jnp.cos(ang_r), jnp.cos(ang_r), jnp.cos(ang_c), jnp.cos(ang_c)], axis=-1)
    sin = jnp.concatenate([-jnp.sin(ang_r), jnp.sin(ang_r), -jnp.sin(ang_c), jnp.sin(ang_c)], axis=-1)
    return cos, sin


def kernel(x, c, ctx, c_ctx, w_mod, b_mod, norm_g, w_ff_in, w_ff_out, ar_w_in, ar_q_g, ar_k_g, ar_conv_w,
           ar_conv_b, ar_wa, ar_ba, ar_wx, ar_bx, ar_lambda, ar_w_out, gm_w_in, gm_b_in, gm_v_g, gm_v_b,
           gm_w_sp, gm_b_sp, gm_w_out):
    bn, s, d = x.shape
    depth = w_mod.shape[0]
    assert depth == 2, "layer pattern implemented for one attention/recurrent layer followed by one gMLP layer"

    rows = -(-(bn + 1) // SUBLANES) * SUBLANES
    cc = jnp.concatenate([c, c_ctx[None, :], jnp.zeros((rows - bn - 1, d), F32)], axis=0)
    mods = _modulation(cc, w_mod, b_mod)

    def lat_mod(layer, k):
        return mods[layer, :bn, k * d:(k + 1) * d].reshape(bn, 1, d)

    def ctx_mod(layer, k):
        return mods[layer, bn:bn + 1, k * d:(k + 1) * d].reshape(1, 1, d)

    g = norm_g.reshape(depth, 4, 1, d)
    w_ff1 = _to_bf16_col_blocks(w_ff_in, FFN_TA)
    w_ff2 = _to_bf16_col_blocks(w_ff_out, FFN_TB)

    w_in = ar_w_in[0].astype(BF16)
    qg = ar_q_g[0].reshape(1, HEAD_DIM)
    kg = ar_k_g[0].reshape(1, HEAD_DIM)
    cos, sin = _rope_tables(s)
    ql, kl, vl, xrl, grl = _in_proj(x, lat_mod(0, 0), lat_mod(0, 1), g[0, 0], w_in, qg, kg, cos, sin, latent=True)
    kc, vc, xrc = _in_proj(ctx, ctx_mod(0, 0), ctx_mod(0, 1), g[0, 0], w_in, qg, kg, None, None, latent=False)

    attn = _attention(ql, kc, kl, vc, vl)

    w_gate = jnp.concatenate([ar_wa[0], ar_wx[0]], axis=-1).astype(BF16)
    b_gate = jnp.concatenate([ar_ba[0].reshape(2, RNN_BLOCKS, 1, RNN_BLOCK_W),
                              ar_bx[0].reshape(2, RNN_BLOCKS, 1, RNN_BLOCK_W)], axis=-1)
    lam = ar_lambda[0].reshape(2, RNN_BLOCKS, 1, RNN_BLOCK_W)
    rnn = _rglru(xrl, xrc, grl, ar_conv_w[0], ar_conv_b[0].reshape(1, D_RNN), w_gate, b_gate, lam)

    x1, h = _out_proj(attn, rnn, ar_w_out[0].astype(BF16), x, lat_mod(0, 2), g[0, 1], g[0, 2],
                      lat_mod(0, 3), lat_mod(0, 4))
    x2, h = _ffn(h, w_ff1, w_ff2, 0, x1, lat_mod(0, 5), g[0, 3], nxt=(g[1, 0], lat_mod(1, 0), lat_mod(1, 1)))

    z = _gm_in(h, gm_w_in[0].astype(BF16), gm_b_in[0].reshape(1, -1))
    x3, h = _gm_out(z, gm_v_g[0].reshape(1, -1), gm_v_b[0].reshape(1, -1), gm_w_sp[0].astype(BF16),
                    gm_b_sp[0].T, gm_w_out[0].astype(BF16), x2, lat_mod(1, 2), g[1, 1], g[1, 2],
                    lat_mod(1, 3), lat_mod(1, 4))
    (x4,) = _ffn(h, w_ff1, w_ff2, 1, x3, lat_mod(1, 5), g[1, 3])
    return x4
```

```python
import functools

import jax
import jax.numpy as jnp
import numpy as np
from jax import lax
from jax.experimental import pallas as pl
from jax.experimental.pallas import tpu as pltpu

F32 = jnp.float32
BF16 = jnp.bfloat16

GRID_W = 64
N_HEADS = 8
N_KV_HEADS = 2
HEAD_DIM = 128
ATTN_W = N_HEADS * HEAD_DIM
KV_W = N_KV_HEADS * HEAD_DIM
ROPE_THETA = 10000.0
ROPE_PAIRS = HEAD_DIM // 4
D_RNN = 1024
RNN_BLOCKS = 8
RNN_BLOCK_W = D_RNN // RNN_BLOCKS
CONV_W = 4
RG_C = 8.0
GM_GROUPS = 16
GM_GROUP_W = 128
CHUNK = 128
EPS = 1e-6

LANES = 128
SUBLANES = 8
VMEM_LIMIT_BYTES = 56 * 1024 * 1024
BIG_VMEM_LIMIT_BYTES = 63 * 1024 * 1024

MOD_TN = 1024
PROJ_TN = 512
PROJ_TM = 512
SCAN_TC = 256
SCAN_TJ = 32
SEG_PAD = 8
ATTN_TQ = 1024
ATTN_SUB = 256
OUT_TM = 1024
EPI_SUB = 256
FFN_TM = 512
FFN_TA = 1024
FFN_TB = 512
GM_TM = 1024
GM_TN = 2048
GM_GROUP_COLS = (1024, 1024)
GMO_TM = 512
CAST_BLOCK_BYTES = 8 * 1024 * 1024


def _params():
    return pltpu.CompilerParams(vmem_limit_bytes=VMEM_LIMIT_BYTES)


def _rms(x, g):
    ms = jnp.mean(x * x, axis=-1, keepdims=True)
    return x * lax.rsqrt(ms + EPS) * g


def _sigmoid(x):
    return 1.0 / (1.0 + jnp.exp(-x))


def _gelu(x):
    c = 0.7978845608028654
    return 0.5 * x * (1.0 + jnp.tanh(c * (x + 0.044715 * (x * x * x))))


def _mod_kernel(cc_ref, w_ref, b_ref, o_ref):
    c = cc_ref[...]
    s = c * _sigmoid(c)
    o_ref[0] = jnp.dot(s.astype(BF16), w_ref[0].astype(BF16), preferred_element_type=F32) + b_ref[0]


def _modulation(cc, w_mod, b_mod):
    depth, d, n = w_mod.shape
    rows = cc.shape[0]
    return pl.pallas_call(
        _mod_kernel,
        grid=(depth, n // MOD_TN),
        in_specs=[
            pl.BlockSpec((rows, d), lambda l, j: (0, 0)),
            pl.BlockSpec((1, d, MOD_TN), lambda l, j: (l, 0, j)),
            pl.BlockSpec((1, 1, MOD_TN), lambda l, j: (l, 0, j)),
        ],
        out_specs=pl.BlockSpec((1, rows, MOD_TN), lambda l, j: (l, 0, j)),
        out_shape=jax.ShapeDtypeStruct((depth, rows, n), F32),
        compiler_params=_params(),
        name="adaln_modulation",
    )(cc, w_mod, b_mod.reshape(depth, 1, n))


def _cast_kernel(w_ref, o_ref):
    tc = o_ref.shape[3]
    for a in range(o_ref.shape[1]):
        o_ref[0, a] = w_ref[0, :, a * tc:(a + 1) * tc].astype(BF16)


def _to_bf16_col_blocks(w, tc):
    layers, r, c = w.shape
    tr = min(r, CAST_BLOCK_BYTES // (c * 4))
    return pl.pallas_call(
        _cast_kernel,
        grid=(layers, r // tr),
        in_specs=[pl.BlockSpec((1, tr, c), lambda l, i: (l, i, 0))],
        out_specs=pl.BlockSpec((1, c // tc, tr, tc), lambda l, i: (l, 0, i, 0)),
        out_shape=jax.ShapeDtypeStruct((layers, c // tc, r, tc), BF16),
        compiler_params=_params(),
        name="weights_to_bf16",
    )(w)


def _head_norm_rope(xh, gain, cos, sin):
    y = _rms(xh, gain)
    if cos is None:
        return y
    lane = lax.broadcasted_iota(jnp.int32, y.shape, 1)
    first_half = (lane % (2 * ROPE_PAIRS)) < ROPE_PAIRS
    partner = jnp.where(first_half,
                        pltpu.roll(y, HEAD_DIM - ROPE_PAIRS, 1),
                        pltpu.roll(y, ROPE_PAIRS, 1))
    return y * cos + partner * sin


def _in_proj_kernel(*refs, latent):
    if latent:
        (x_ref, sh_ref, sc_ref, g_ref, w_ref, qg_ref, kg_ref, cos_ref, sin_ref,
         q_ref, k_ref, v_ref, xr_ref, gr_ref, h_scr) = refs
        cos, sin = cos_ref[...], sin_ref[...]
    else:
        (x_ref, sh_ref, sc_ref, g_ref, w_ref, kg_ref, k_ref, v_ref, xr_ref, h_scr) = refs
        cos = sin = None
    h = _rms(x_ref[0], g_ref[...] * (1.0 + sc_ref[0])) + sh_ref[0]
    h_scr[...] = h.astype(BF16)

    def project(c0, width):
        return jnp.dot(h_scr[...], w_ref[:, c0:c0 + width], preferred_element_type=F32)

    if latent:
        for c0 in range(0, ATTN_W, PROJ_TN):
            acc = project(c0, PROJ_TN)
            for hh in range(PROJ_TN // HEAD_DIM):
                sl = slice(hh * HEAD_DIM, (hh + 1) * HEAD_DIM)
                dst = slice(c0 + hh * HEAD_DIM, c0 + (hh + 1) * HEAD_DIM)
                q_ref[0, :, dst] = _head_norm_rope(acc[:, sl], qg_ref[...], cos, sin).astype(BF16)

    acc = project(ATTN_W, 2 * KV_W)
    for hh in range(N_KV_HEADS):
        sl = slice(hh * HEAD_DIM, (hh + 1) * HEAD_DIM)
        k_ref[0, :, sl] = _head_norm_rope(acc[:, sl], kg_ref[...], cos, sin).astype(BF16)
    v_ref[0] = acc[:, KV_W:2 * KV_W].astype(BF16)

    rnn0 = ATTN_W + 2 * KV_W
    for c0 in range(0, D_RNN, PROJ_TN):
        xr_ref[0, :, c0:c0 + PROJ_TN] = project(rnn0 + c0, PROJ_TN)
    if latent:
        for c0 in range(0, D_RNN, PROJ_TN):
            gr_ref[0, :, c0:c0 + PROJ_TN] = project(rnn0 + D_RNN + c0, PROJ_TN)


def _in_proj(x, shift, scale, g, w, qg, kg, cos, sin, *, latent):
    bn, length, d = x.shape
    tm = min(PROJ_TM, length)
    per_batch = shift.shape[0] > 1
    mod_map = (lambda b, i: (b, 0, 0)) if per_batch else (lambda b, i: (0, 0, 0))
    vec_map = lambda b, i: (0, 0)
    row = lambda b, i: (b, i, 0)

    in_specs = [
        pl.BlockSpec((1, tm, d), row),
        pl.BlockSpec((1, 1, d), mod_map),
        pl.BlockSpec((1, 1, d), mod_map),
        pl.BlockSpec((1, d), vec_map),
        pl.BlockSpec(w.shape, vec_map),
    ]
    args = [x, shift, scale, g, w]
    k_spec = pl.BlockSpec((1, tm, KV_W), row)
    xr_spec = pl.BlockSpec((1, tm, D_RNN), row)
    k_shape = jax.ShapeDtypeStruct((bn, length, KV_W), BF16)
    xr_shape = jax.ShapeDtypeStruct((bn, length, D_RNN), F32)
    if latent:
        in_specs += [pl.BlockSpec((1, HEAD_DIM), vec_map), pl.BlockSpec((1, HEAD_DIM), vec_map),
                     pl.BlockSpec((tm, HEAD_DIM), lambda b, i: (i, 0)),
                     pl.BlockSpec((tm, HEAD_DIM), lambda b, i: (i, 0))]
        args += [qg, kg, cos, sin]
        out_specs = [pl.BlockSpec((1, tm, ATTN_W), row), k_spec, k_spec, xr_spec, xr_spec]
        out_shape = [jax.ShapeDtypeStruct((bn, length, ATTN_W), BF16), k_shape, k_shape, xr_shape, xr_shape]
    else:
        in_specs += [pl.BlockSpec((1, HEAD_DIM), vec_map)]
        args += [kg]
        out_specs = [k_spec, k_spec, xr_spec]
        out_shape = [k_shape, k_shape, xr_shape]

    return pl.pallas_call(
        functools.partial(_in_proj_kernel, latent=latent),
        grid=(bn, length // tm),
        in_specs=in_specs,
        out_specs=out_specs,
        out_shape=out_shape,
        scratch_shapes=[pltpu.VMEM((tm, d), BF16)],
        compiler_params=_params(),
        name="in_proj_latent" if latent else "in_proj_context",
    )(*args)


def _dwconv(x, w, b):
    n = x.shape[0]
    row = lax.broadcasted_iota(jnp.int32, x.shape, 0)
    y = b + x * w[2:3]
    y = y + jnp.where(row >= 2, pltpu.roll(x, 2, 0), 0.0) * w[0:1]
    y = y + jnp.where(row >= 1, pltpu.roll(x, 1, 0), 0.0) * w[1:2]
    y = y + jnp.where(row < n - 1, pltpu.roll(x, n - 1, 0), 0.0) * w[3:4]
    return y


def _scan_chunk(a, b, carry, reverse):
    tc = a.shape[0]
    nv = tc // SUBLANES
    a3 = a.reshape(nv, SUBLANES, LANES)
    b3 = b.reshape(nv, SUBLANES, LANES)
    row = lax.broadcasted_iota(jnp.int32, a3.shape, 1)
    step = 1
    while step < SUBLANES:
        if reverse:
            shift, valid = SUBLANES - step, row < SUBLANES - step
        else:
            shift, valid = step, row >= step
        a_sh = pltpu.roll(a3, shift, 1)
        b_sh = pltpu.roll(b3, shift, 1)
        b3 = jnp.where(valid, a3 * b_sh + b3, b3)
        a3 = jnp.where(valid, a3 * a_sh, a3)
        step *= 2
    last = 0 if reverse else SUBLANES - 1
    a_last = jnp.broadcast_to(a3[:, last:last + 1, :], a3.shape)
    b_last = jnp.broadcast_to(b3[:, last:last + 1, :], b3.shape)
    hs = [None] * nv
    for v in (range(nv - 1, -1, -1) if reverse else range(nv)):
        hs[v] = b3[v] + a3[v] * carry
        carry = b_last[v] + a_last[v] * carry
    return hs, carry


def _rglru_kernel(xl_ref, xc_ref, gl_ref, cw_ref, cb_ref, w_ref, bias_ref, lam_ref, o_ref,
                  xpad, xconv_c, xcs, hloc_f, aloc_f, hloc_b, aloc_b, hsum):
    cw = cw_ref[...]
    cb = cb_ref[...]
    length = xl_ref.shape[1]
    tseg = length // SUBLANES
    pitch = tseg + SEG_PAD
    n_j = tseg // SCAN_TJ
    n_ctx = xc_ref.shape[1] // SCAN_TC

    xconv_c[...] = _dwconv(xc_ref[0], cw, cb)

    pad_zeros = jnp.zeros((SEG_PAD, LANES), F32)
    xpad[0:SEG_PAD, :] = pad_zeros
    for s in range(SUBLANES):
        base = SEG_PAD + s * pitch
        xpad[base:base + tseg, :] = xl_ref[0, s * tseg:(s + 1) * tseg, :]
        pad = base + tseg
        xpad[pad:pad + SEG_PAD, :] = pad_zeros
        if s + 1 < SUBLANES:
            xpad[pad:pad + 1, :] = xl_ref[0, (s + 1) * tseg:(s + 1) * tseg + 1, :]
        xpad[pad + SEG_PAD - 2:pad + SEG_PAD, :] = xl_ref[0, (s + 1) * tseg - 2:(s + 1) * tseg, :]

    taps = [jnp.broadcast_to(cw[t:t + 1], (SUBLANES, LANES)) for t in range(CONV_W)]
    bias_rows = jnp.broadcast_to(cb, (SUBLANES, LANES))

    def conv_step(kk, _):
        j0 = kk * SCAN_TJ
        xs = [xpad[pl.ds(SEG_PAD + j0 - CONV_W // 2 + t, SUBLANES, stride=pitch), :]
              for t in range(SCAN_TJ + CONV_W - 1)]
        for jj in range(SCAN_TJ):
            y = bias_rows
            for t in range(CONV_W):
                y = y + xs[jj + t] * taps[t]
            xcs[pl.ds(pl.multiple_of((j0 + jj) * SUBLANES, SUBLANES), SUBLANES), :] = y
        return 0

    lax.fori_loop(0, n_j, conv_step, 0)

    def gates(xc, d):
        z = jnp.dot(xc.astype(BF16), w_ref[d, 0], preferred_element_type=F32) + bias_ref[d, 0]
        t_r = jnp.tanh(0.5 * z[:, :RNN_BLOCK_W])
        i = 0.5 * jnp.tanh(0.5 * z[:, RNN_BLOCK_W:]) + 0.5
        lam = lam_ref[d, 0]
        softplus_neg_lam = jnp.maximum(-lam, 0.0) + jnp.log1p(jnp.exp(-jnp.abs(lam)))
        half_rate = (-0.5 * RG_C) * softplus_neg_lam
        log_a = half_rate * t_r + half_rate
        a = jnp.exp(log_a)
        th = jnp.tanh(log_a)
        b = jnp.sqrt(-2.0 * th / (1.0 - th)) * (i * xc)
        return a, b

    ctx_state = []
    for d, reverse in ((0, False), (1, True)):
        def ctx_step(kk, carry, d=d, reverse=reverse):
            k = (n_ctx - 1 - kk) if reverse else kk
            r0 = pl.multiple_of(k * SCAN_TC, SCAN_TC)
            a, b = gates(xconv_c[pl.ds(r0, SCAN_TC), :], d)
            _, carry = _scan_chunk(a, b, carry, reverse)
            return carry

        ctx_state.append(lax.fori_loop(0, n_ctx, ctx_step, jnp.zeros((SUBLANES, LANES), F32)))

    def local_step(kk, carry):
        new = []
        for d, (h, acc_a) in enumerate(carry):
            reverse = d == 1
            j0 = ((n_j - 1 - kk) if reverse else kk) * SCAN_TJ
            src = pl.ds(pl.multiple_of(j0 * SUBLANES, SCAN_TJ * SUBLANES), SCAN_TJ * SUBLANES)
            a, b = gates(xcs[src, :], d)
            hloc, aloc = (hloc_b, aloc_b) if reverse else (hloc_f, aloc_f)
            for jj in (range(SCAN_TJ - 1, -1, -1) if reverse else range(SCAN_TJ)):
                rows = slice(jj * SUBLANES, (jj + 1) * SUBLANES)
                h = a[rows] * h + b[rows]
                acc_a = a[rows] * acc_a
                dst = pl.ds(pl.multiple_of((j0 + jj) * SUBLANES, SUBLANES), SUBLANES)
                hloc[dst, :] = h
                aloc[dst, :] = acc_a
            new.append((h, acc_a))
        return tuple(new)

    zeros = jnp.zeros((SUBLANES, LANES), F32)
    ones = jnp.ones((SUBLANES, LANES), F32)
    (h_f, a_f), (h_b, a_b) = lax.fori_loop(0, n_j, local_step, ((zeros, ones), (zeros, ones)))

    def entry_states(h_end, a_end, c_in, reverse):
        row = lax.broadcasted_iota(jnp.int32, (SUBLANES, LANES), 0)
        first = SUBLANES - 1 if reverse else 0
        c = c_in[0:1]
        out = jnp.where(row == first, jnp.broadcast_to(c, (SUBLANES, LANES)), 0.0)
        for k in range(1, SUBLANES):
            s = first - k if reverse else first + k
            prev = s + 1 if reverse else s - 1
            c = h_end[prev:prev + 1] + a_end[prev:prev + 1] * c
            out = jnp.where(row == s, jnp.broadcast_to(c, (SUBLANES, LANES)), out)
        return out

    c_f = entry_states(h_f, a_f, ctx_state[0], False)
    c_b = entry_states(h_b, a_b, ctx_state[1], True)

    def fix_step(kk, _):
        j0 = kk * SCAN_TJ
        for jj in range(SCAN_TJ):
            src = pl.ds(pl.multiple_of((j0 + jj) * SUBLANES, SUBLANES), SUBLANES)
            h = (hloc_f[src, :] + aloc_f[src, :] * c_f) + (hloc_b[src, :] + aloc_b[src, :] * c_b)
            hsum[pl.ds(j0 + jj, SUBLANES, stride=pitch), :] = h
        return 0

    lax.fori_loop(0, n_j, fix_step, 0)

    for s in range(SUBLANES):
        for r0 in range(0, tseg, SCAN_TC):
            src = slice(s * pitch + r0, s * pitch + r0 + SCAN_TC)
            dst = slice(s * tseg + r0, s * tseg + r0 + SCAN_TC)
            o_ref[0, dst, :] = (hsum[src, :] * _gelu(gl_ref[0, dst, :])).astype(BF16)


def _rglru(xl, xc, gl, conv_w, conv_b, w_gate, b_gate, lam):
    bn, s, _ = xl.shape
    ctx_len = xc.shape[1]
    bw = RNN_BLOCK_W
    padded = s + SUBLANES * SEG_PAD
    col = lambda b, n: (b, 0, n)
    return pl.pallas_call(
        _rglru_kernel,
        grid=(bn, RNN_BLOCKS),
        in_specs=[
            pl.BlockSpec((1, s, bw), col),
            pl.BlockSpec((1, ctx_len, bw), col),
            pl.BlockSpec((1, s, bw), col),
            pl.BlockSpec((CONV_W, bw), lambda b, n: (0, n)),
            pl.BlockSpec((1, bw), lambda b, n: (0, n)),
            pl.BlockSpec((2, 1, bw, 2 * bw), lambda b, n: (0, n, 0, 0)),
            pl.BlockSpec((2, 1, 1, 2 * bw), lambda b, n: (0, n, 0, 0)),
            pl.BlockSpec((2, 1, 1, bw), lambda b, n: (0, n, 0, 0)),
        ],
        out_specs=pl.BlockSpec((1, s, bw), col),
        out_shape=jax.ShapeDtypeStruct((bn, s, D_RNN), BF16),
        scratch_shapes=[
            pltpu.VMEM((SEG_PAD + padded, bw), F32),
            pltpu.VMEM((ctx_len, bw), F32),
            pltpu.VMEM((s, bw), F32),
            pltpu.VMEM((s, bw), F32), pltpu.VMEM((s, bw), F32),
            pltpu.VMEM((s, bw), F32), pltpu.VMEM((s, bw), F32),
            pltpu.VMEM((padded, bw), F32),
        ],
        compiler_params=_params(),
        name="rglru_bidirectional",
    )(xl, xc, gl, conv_w, conv_b, w_gate, b_gate, lam)


def _attn_kernel(q_ref, kc_ref, kl_ref, vc_ref, vl_ref, o_ref, k_scr, v_scr):
    n_ctx = kc_ref.shape[1]

    @pl.when(pl.program_id(2) == 0)
    def _():
        k_scr[0:n_ctx, :] = kc_ref[0]
        k_scr[n_ctx:, :] = kl_ref[0]
        v_scr[0:n_ctx, 0:HEAD_DIM] = vc_ref[0]
        v_scr[n_ctx:, 0:HEAD_DIM] = vl_ref[0]
        v_scr[:, HEAD_DIM:] = jnp.ones((v_scr.shape[0], HEAD_DIM), BF16)

    k = k_scr[...]
    v = v_scr[...]
    c = (HEAD_DIM ** -0.5) * 1.4426950408889634
    for r0 in range(0, q_ref.shape[1], ATTN_SUB):
        rows = slice(r0, r0 + ATTN_SUB)
        for hh in range(N_HEADS // N_KV_HEADS):
            sl = slice(hh * HEAD_DIM, (hh + 1) * HEAD_DIM)
            s = lax.dot_general(q_ref[0, rows, sl], k, (((1,), (1,)), ((), ())), preferred_element_type=F32)
            m = jnp.max(s, axis=-1, keepdims=True)
            p = jnp.exp2((s - m) * c).astype(BF16)
            ov = jnp.dot(p, v, preferred_element_type=F32)
            o_ref[0, rows, sl] = (ov[:, :HEAD_DIM] / ov[:, HEAD_DIM:HEAD_DIM + 1]).astype(BF16)


def _attention(q, kc, kl, vc, vl):
    bn, s, _ = q.shape
    n_ctx, n_lat = kc.shape[1], kl.shape[1]
    t = n_ctx + n_lat
    gw = ATTN_W // N_KV_HEADS
    kv_spec = lambda n: pl.BlockSpec((1, n, HEAD_DIM), lambda b, h, i: (b, 0, h))
    return pl.pallas_call(
        _attn_kernel,
        grid=(bn, N_KV_HEADS, s // ATTN_TQ),
        in_specs=[
            pl.BlockSpec((1, ATTN_TQ, gw), lambda b, h, i: (b, i, h)),
            kv_spec(n_ctx), kv_spec(n_lat), kv_spec(n_ctx), kv_spec(n_lat),
        ],
        out_specs=pl.BlockSpec((1, ATTN_TQ, gw), lambda b, h, i: (b, i, h)),
        out_shape=jax.ShapeDtypeStruct((bn, s, ATTN_W), BF16),
        scratch_shapes=[pltpu.VMEM((t, HEAD_DIM), BF16), pltpu.VMEM((t, 2 * HEAD_DIM), BF16)],
        compiler_params=_params(),
        name="gqa_attention",
    )(q, kc, kl, vc, vl)


def _residual_and_next(x, mix, gate, g_post, g_next, shift_next, scale_next):
    x1 = x + _rms(mix, gate * g_post)
    h = _rms(x1, g_next * (1.0 + scale_next)) + shift_next
    return x1, h.astype(BF16)


def _out_proj_kernel(a1_ref, a2_ref, w_ref, x_ref, gate_ref, g1_ref, g2_ref, sh_ref, sc_ref, x1_ref, h_ref):
    ka = a1_ref.shape[2]
    for r0 in range(0, x_ref.shape[1], EPI_SUB):
        rows = slice(r0, r0 + EPI_SUB)
        mix = jnp.dot(a1_ref[0, rows, :], w_ref[0:ka, :], preferred_element_type=F32)
        mix = mix + jnp.dot(a2_ref[0, rows, :], w_ref[ka:, :], preferred_element_type=F32)
        x1, h = _residual_and_next(x_ref[0, rows, :], mix, gate_ref[0], g1_ref[...], g2_ref[...],
                                   sh_ref[0], sc_ref[0])
        x1_ref[0, rows, :] = x1
        h_ref[0, rows, :] = h


def _out_proj(a1, a2, w, x, gate, g1, g2, shift, scale):
    bn, s, d = x.shape
    tm = OUT_TM
    row = lambda b, i: (b, i, 0)
    mod = lambda b, i: (b, 0, 0)
    vec = lambda b, i: (0, 0)
    return pl.pallas_call(
        _out_proj_kernel,
        grid=(bn, s // tm),
        in_specs=[
            pl.BlockSpec((1, tm, a1.shape[2]), row),
            pl.BlockSpec((1, tm, a2.shape[2]), row),
            pl.BlockSpec(w.shape, vec, pipeline_mode=pl.Buffered(1)),
            pl.BlockSpec((1, tm, d), row),
            pl.BlockSpec((1, 1, d), mod),
            pl.BlockSpec((1, d), vec),
            pl.BlockSpec((1, d), vec),
            pl.BlockSpec((1, 1, d), mod),
            pl.BlockSpec((1, 1, d), mod),
        ],
        out_specs=[pl.BlockSpec((1, tm, d), row), pl.BlockSpec((1, tm, d), row)],
        out_shape=[jax.ShapeDtypeStruct((bn, s, d), F32), jax.ShapeDtypeStruct((bn, s, d), BF16)],
        compiler_params=pltpu.CompilerParams(vmem_limit_bytes=BIG_VMEM_LIMIT_BYTES),
        name="out_proj_residual",
    )(a1, a2, w, x, gate, g1, g2, shift, scale)


def _ffn_kernel(*refs, has_next, na):
    if has_next:
        (h_ref, w1_ref, w2_ref, x_ref, gate_ref, g3_ref, gn_ref, sh_ref, sc_ref,
         x2_ref, hn_ref, t_scr) = refs
    else:
        h_ref, w1_ref, w2_ref, x_ref, gate_ref, g3_ref, x2_ref, t_scr = refs
    j = pl.program_id(2)
    ta = t_scr.shape[2]
    tb = w2_ref.shape[1]

    @pl.when(j < na)
    def _():
        t = jnp.dot(h_ref[0], w1_ref[...], preferred_element_type=F32)
        t = jnp.maximum(t, 0.0)
        t_scr[j] = (t * t).astype(BF16)

    last = pl.num_programs(2) - 1

    def down_proj(rows):
        y = jnp.dot(t_scr[0, rows, :], w2_ref[0:ta, :], preferred_element_type=F32)
        for a in range(1, na):
            y = y + jnp.dot(t_scr[a, rows, :], w2_ref[a * ta:(a + 1) * ta, :], preferred_element_type=F32)
        return y

    @pl.when(jnp.logical_and(j >= na, j < last))
    def _():
        col = pl.multiple_of((j - na) * tb, tb)
        x2_ref[0, :, pl.ds(col, tb)] = down_proj(slice(None))

    @pl.when(j == last)
    def _():
        col0 = x2_ref.shape[2] - tb
        for r0 in range(0, x2_ref.shape[1], EPI_SUB):
            rows = slice(r0, r0 + EPI_SUB)
            x2_ref[0, rows, col0:] = down_proj(rows)
            if has_next:
                x2, hn = _residual_and_next(x_ref[0, rows, :], x2_ref[0, rows, :], gate_ref[0], g3_ref[...],
                                            gn_ref[...], sh_ref[0], sc_ref[0])
                x2_ref[0, rows, :] = x2
                hn_ref[0, rows, :] = hn
            else:
                x2_ref[0, rows, :] = x_ref[0, rows, :] + _rms(x2_ref[0, rows, :], gate_ref[0] * g3_ref[...])


def _ffn(h, w1, w2, layer, x, gate, g3, nxt=None):
    bn, s, d = x.shape
    tm = FFN_TM
    na, ta = w1.shape[1], w1.shape[3]
    nb, tb = w2.shape[1], w2.shape[3]
    dff = na * ta
    row = lambda b, i, j: (b, i, 0)
    mod = lambda b, i, j: (b, 0, 0)
    vec = lambda b, i, j: (0, 0)
    in_specs = [
        pl.BlockSpec((1, tm, d), row),
        pl.BlockSpec((None, None, d, ta), lambda b, i, j: (layer, jnp.minimum(j, na - 1), 0, 0)),
        pl.BlockSpec((None, None, dff, tb), lambda b, i, j: (layer, jnp.maximum(j - na, 0), 0, 0)),
        pl.BlockSpec((1, tm, d), row),
        pl.BlockSpec((1, 1, d), mod),
        pl.BlockSpec((1, d), vec),
    ]
    args = [h, w1, w2, x, gate, g3]
    out_specs = [pl.BlockSpec((1, tm, d), row)]
    out_shape = [jax.ShapeDtypeStruct((bn, s, d), F32)]
    if nxt is not None:
        in_specs += [pl.BlockSpec((1, d), vec), pl.BlockSpec((1, 1, d), mod), pl.BlockSpec((1, 1, d), mod)]
        args += list(nxt)
        out_specs.append(pl.BlockSpec((1, tm, d), row))
        out_shape.append(jax.ShapeDtypeStruct((bn, s, d), BF16))
    return pl.pallas_call(
        functools.partial(_ffn_kernel, has_next=nxt is not None, na=na),
        grid=(bn, s // tm, na + nb),
        in_specs=in_specs,
        out_specs=out_specs,
        out_shape=out_shape,
        scratch_shapes=[pltpu.VMEM((na, tm, ta), BF16)],
        compiler_params=pltpu.CompilerParams(vmem_limit_bytes=BIG_VMEM_LIMIT_BYTES),
        name="sq_relu_mlp",
    )(*args)


def _gm_in_kernel(h_ref, w_ref, b_ref, z_ref):
    c0 = 0
    for width in GM_GROUP_COLS:
        cols = slice(c0, c0 + width)
        z = jnp.dot(h_ref[0], w_ref[:, cols], preferred_element_type=F32) + b_ref[:, cols]
        z_ref[0, :, cols] = _gelu(z)
        c0 += width


def _gm_in(h, w, b):
    bn, s, d = h.shape
    n = w.shape[1]
    return pl.pallas_call(
        _gm_in_kernel,
        grid=(bn, s // GM_TM, n // GM_TN),
        in_specs=[
            pl.BlockSpec((1, GM_TM, d), lambda b_, i, j: (b_, i, 0)),
            pl.BlockSpec((d, GM_TN), lambda b_, i, j: (0, j)),
            pl.BlockSpec((1, GM_TN), lambda b_, i, j: (0, j)),
        ],
        out_specs=pl.BlockSpec((1, GM_TM, GM_TN), lambda b_, i, j: (b_, i, j)),
        out_shape=jax.ShapeDtypeStruct((bn, s, n), F32),
        compiler_params=_params(),
        name="gmlp_in_proj",
    )(h, w, b)


def _gm_out_kernel(u_ref, v_ref, vg_ref, vb_ref, wsp_ref, bsp_ref, w_ref, x_ref, gate_ref, g1_ref, g2_ref,
                   sh_ref, sc_ref, x1_ref, h_ref, vn_scr, gated_scr):
    for r0 in range(0, x_ref.shape[1], EPI_SUB):
        sub = slice(r0, r0 + EPI_SUB)
        v = v_ref[0, sub, :]
        mu = jnp.mean(v, axis=-1, keepdims=True)
        vc = v - mu
        var = jnp.mean(vc * vc, axis=-1, keepdims=True)
        vn_scr[sub, :] = (vc * lax.rsqrt(var + EPS) * vg_ref[...] + vb_ref[...]).astype(BF16)
        for c0 in range(r0, r0 + EPI_SUB, CHUNK):
            rows = slice(c0, c0 + CHUNK)
            for g in range(GM_GROUPS):
                cols = slice(g * GM_GROUP_W, (g + 1) * GM_GROUP_W)
                sv = jnp.dot(wsp_ref[g], vn_scr[rows, cols], preferred_element_type=F32) + bsp_ref[:, g:g + 1]
                gated_scr[rows, cols] = (u_ref[0, rows, cols] * sv).astype(BF16)
        mix = jnp.dot(gated_scr[sub, :], w_ref[...], preferred_element_type=F32)
        x1, h = _residual_and_next(x_ref[0, sub, :], mix, gate_ref[0], g1_ref[...], g2_ref[...],
                                   sh_ref[0], sc_ref[0])
        x1_ref[0, sub, :] = x1
        h_ref[0, sub, :] = h


def _gm_out(z, vg, vb, wsp, bsp_t, w, x, gate, g1, g2, shift, scale):
    bn, s, d = x.shape
    dg = z.shape[2] // 2
    tm = GMO_TM
    row = lambda b, i: (b, i, 0)
    mod = lambda b, i: (b, 0, 0)
    vec = lambda b, i: (0, 0)
    return pl.pallas_call(
        _gm_out_kernel,
        grid=(bn, s // tm),
        in_specs=[
            pl.BlockSpec((1, tm, dg), lambda b, i: (b, i, 0)),
            pl.BlockSpec((1, tm, dg), lambda b, i: (b, i, 1)),
            pl.BlockSpec((1, dg), vec),
            pl.BlockSpec((1, dg), vec),
            pl.BlockSpec(wsp.shape, lambda b, i: (0, 0, 0), pipeline_mode=pl.Buffered(1)),
            pl.BlockSpec(bsp_t.shape, vec),
            pl.BlockSpec(w.shape, vec, pipeline_mode=pl.Buffered(1)),
            pl.BlockSpec((1, tm, d), row),
            pl.BlockSpec((1, 1, d), mod),
            pl.BlockSpec((1, d), vec),
            pl.BlockSpec((1, d), vec),
            pl.BlockSpec((1, 1, d), mod),
            pl.BlockSpec((1, 1, d), mod),
        ],
        out_specs=[pl.BlockSpec((1, tm, d), row), pl.BlockSpec((1, tm, d), row)],
        out_shape=[jax.ShapeDtypeStruct((bn, s, d), F32), jax.ShapeDtypeStruct((bn, s, d), BF16)],
        scratch_shapes=[pltpu.VMEM((tm, dg), BF16), pltpu.VMEM((tm, dg), BF16)],
        compiler_params=_params(),
        name="gmlp_spatial_out_proj",
    )(z, z, vg, vb, wsp, bsp_t, w, x, gate, g1, g2, shift, scale)


def _rope_tables(n):
    t = np.arange(n)
    r_idx = (t // GRID_W).astype(np.float32)
    c_idx = (t % GRID_W).astype(np.float32)
    freqs = np.float32(ROPE_THETA) ** (-np.arange(ROPE_PAIRS, dtype=np.float32) / np.float32(ROPE_PAIRS))
    ang_r = r_idx[:, None] * freqs
    ang_c = c_idx[:, None] * freqs
    cos = np.concatenate([np.cos(ang_r), np.cos(ang_r), np.cos(ang_c), np.cos(ang_c)], axis=-1)
    sin = np.concatenate([-np.sin(ang_r), np.sin(ang_r), -np.sin(ang_c), np.sin(ang_c)], axis=-1)
    return jnp.asarray(cos, F32), jnp.asarray(sin, F32)


def kernel(x, c, ctx, c_ctx, w_mod, b_mod, norm_g, w_ff_in, w_ff_out, ar_w_in, ar_q_g, ar_k_g, ar_conv_w,
           ar_conv_b, ar_wa, ar_ba, ar_wx, ar_bx, ar_lambda, ar_w_out, gm_w_in, gm_b_in, gm_v_g, gm_v_b,
           gm_w_sp, gm_b_sp, gm_w_out):
    bn, s, d = x.shape
    depth = w_mod.shape[0]
    assert depth == 2, "layer pattern implemented for one attention/recurrent layer followed by one gMLP layer"

    rows = -(-(bn + 1) // SUBLANES) * SUBLANES
    cc = jnp.concatenate([c, c_ctx[None, :], jnp.zeros((rows - bn - 1, d), F32)], axis=0)
    mods = _modulation(cc, w_mod, b_mod)

    def lat_mod(layer, k):
        return mods[layer, :bn, k * d:(k + 1) * d].reshape(bn, 1, d)

    def ctx_mod(layer, k):
        return mods[layer, bn:bn + 1, k * d:(k + 1) * d].reshape(1, 1, d)

    g = norm_g.reshape(depth, 4, 1, d)
    w_ff1 = _to_bf16_col_blocks(w_ff_in, FFN_TA)
    w_ff2 = _to_bf16_col_blocks(w_ff_out, FFN_TB)

    w_in = ar_w_in[0].astype(BF16)
    qg = ar_q_g[0].reshape(1, HEAD_DIM)
    kg = ar_k_g[0].reshape(1, HEAD_DIM)
    cos, sin = _rope_tables(s)
    ql, kl, vl, xrl, grl = _in_proj(x, lat_mod(0, 0), lat_mod(0, 1), g[0, 0], w_in, qg, kg, cos, sin, latent=True)
    kc, vc, xrc = _in_proj(ctx, ctx_mod(0, 0), ctx_mod(0, 1), g[0, 0], w_in, qg, kg, None, None, latent=False)

    attn = _attention(ql, kc, kl, vc, vl)

    w_gate = jnp.concatenate([ar_wa[0], ar_wx[0]], axis=-1).astype(BF16)
    b_gate = jnp.concatenate([ar_ba[0].reshape(2, RNN_BLOCKS, 1, RNN_BLOCK_W),
                              ar_bx[0].reshape(2, RNN_BLOCKS, 1, RNN_BLOCK_W)], axis=-1)
    lam = ar_lambda[0].reshape(2, RNN_BLOCKS, 1, RNN_BLOCK_W)
    rnn = _rglru(xrl, xrc, grl, ar_conv_w[0], ar_conv_b[0].reshape(1, D_RNN), w_gate, b_gate, lam)

    x1, h = _out_proj(attn, rnn, ar_w_out[0].astype(BF16), x, lat_mod(0, 2), g[0, 1], g[0, 2],
                      lat_mod(0, 3), lat_mod(0, 4))
    x2, h = _ffn(h, w_ff1, w_ff2, 0, x1, lat_mod(0, 5), g[0, 3], nxt=(g[1, 0], lat_mod(1, 0), lat_mod(1, 1)))

    z = _gm_in(h, gm_w_in[0].astype(BF16), gm_b_in[0].reshape(1, -1))
    x3, h = _gm_out(z, gm_v_g[0].reshape(1, -1), gm_v_b[0].reshape(1, -1), gm_w_sp[0].astype(BF16),
                    gm_b_sp[0].T, gm_w_out[0].astype(BF16), x2, lat_mod(1, 2), g[1, 1], g[1, 2],
                    lat_mod(1, 3), lat_mod(1, 4))
    (x4,) = _ffn(h, w_ff1, w_ff2, 1, x3, lat_mod(1, 5), g[1, 3])
    return x4
```

```python
import functools

import jax
import jax.numpy as jnp
import numpy as np
from jax import lax
from jax.experimental import pallas as pl
from jax.experimental.pallas import tpu as pltpu

F32 = jnp.float32
BF16 = jnp.bfloat16

GRID_W = 64
N_HEADS = 8
N_KV_HEADS = 2
HEAD_DIM = 128
ATTN_W = N_HEADS * HEAD_DIM
KV_W = N_KV_HEADS * HEAD_DIM
ROPE_THETA = 10000.0
ROPE_PAIRS = HEAD_DIM // 4
D_RNN = 1024
RNN_BLOCKS = 8
RNN_BLOCK_W = D_RNN // RNN_BLOCKS
CONV_W = 4
RG_C = 8.0
GM_GROUPS = 16
GM_GROUP_W = 128
CHUNK = 128
EPS = 1e-6

LANES = 128
SUBLANES = 8
VMEM_LIMIT_BYTES = 56 * 1024 * 1024
BIG_VMEM_LIMIT_BYTES = 63 * 1024 * 1024

MOD_TN = 1024
PROJ_TN = 512
PROJ_TM = 512
SCAN_TC = 256
SCAN_TJ = 32
SEG_PAD = 8
ATTN_TQ = 1024
ATTN_SUB = 256
OUT_TM = 512
EPI_SUB = 256
FFN_TM = 512
FFN_TA = 1024
FFN_TB = 512
GM_TM = 1024
GM_TN = 2048
GM_GROUP_COLS = (1024, 1024)
GMO_TM = 512
CAST_BLOCK_BYTES = 8 * 1024 * 1024


def _params():
    return pltpu.CompilerParams(vmem_limit_bytes=VMEM_LIMIT_BYTES)


def _rms(x, g):
    ms = jnp.mean(x * x, axis=-1, keepdims=True)
    return x * lax.rsqrt(ms + EPS) * g


def _sigmoid(x):
    return 1.0 / (1.0 + jnp.exp(-x))


def _gelu(x):
    c = 0.7978845608028654
    return 0.5 * x * (1.0 + jnp.tanh(c * (x + 0.044715 * (x * x * x))))


def _mod_kernel(cc_ref, w_ref, b_ref, o_ref):
    c = cc_ref[...]
    s = c * _sigmoid(c)
    o_ref[0] = jnp.dot(s.astype(BF16), w_ref[0].astype(BF16), preferred_element_type=F32) + b_ref[0]


def _modulation(cc, w_mod, b_mod):
    depth, d, n = w_mod.shape
    rows = cc.shape[0]
    return pl.pallas_call(
        _mod_kernel,
        grid=(depth, n // MOD_TN),
        in_specs=[
            pl.BlockSpec((rows, d), lambda l, j: (0, 0)),
            pl.BlockSpec((1, d, MOD_TN), lambda l, j: (l, 0, j)),
            pl.BlockSpec((1, 1, MOD_TN), lambda l, j: (l, 0, j)),
        ],
        out_specs=pl.BlockSpec((1, rows, MOD_TN), lambda l, j: (l, 0, j)),
        out_shape=jax.ShapeDtypeStruct((depth, rows, n), F32),
        compiler_params=_params(),
        name="adaln_modulation",
    )(cc, w_mod, b_mod.reshape(depth, 1, n))


def _cast_kernel(w_ref, o_ref):
    tc = o_ref.shape[3]
    for a in range(o_ref.shape[1]):
        o_ref[0, a] = w_ref[0, :, a * tc:(a + 1) * tc].astype(BF16)


def _to_bf16_col_blocks(w, tc):
    layers, r, c = w.shape
    tr = min(r, CAST_BLOCK_BYTES // (c * 4))
    return pl.pallas_call(
        _cast_kernel,
        grid=(layers, r // tr),
        in_specs=[pl.BlockSpec((1, tr, c), lambda l, i: (l, i, 0))],
        out_specs=pl.BlockSpec((1, c // tc, tr, tc), lambda l, i: (l, 0, i, 0)),
        out_shape=jax.ShapeDtypeStruct((layers, c // tc, r, tc), BF16),
        compiler_params=_params(),
        name="weights_to_bf16",
    )(w)


def _head_norm_rope(xh, gain, cos, sin):
    y = _rms(xh, gain)
    if cos is None:
        return y
    lane = lax.broadcasted_iota(jnp.int32, y.shape, 1)
    first_half = (lane % (2 * ROPE_PAIRS)) < ROPE_PAIRS
    partner = jnp.where(first_half,
                        pltpu.roll(y, HEAD_DIM - ROPE_PAIRS, 1),
                        pltpu.roll(y, ROPE_PAIRS, 1))
    return y * cos + partner * sin


def _in_proj_kernel(*refs, latent):
    if latent:
        (x_ref, sh_ref, sc_ref, g_ref, w_ref, qg_ref, kg_ref, cos_ref, sin_ref,
         q_ref, k_ref, v_ref, xr_ref, gr_ref, h_scr) = refs
        cos, sin = cos_ref[...], sin_ref[...]
    else:
        (x_ref, sh_ref, sc_ref, g_ref, w_ref, kg_ref, k_ref, v_ref, xr_ref, h_scr) = refs
        cos = sin = None
    h = _rms(x_ref[0], g_ref[...] * (1.0 + sc_ref[0])) + sh_ref[0]
    h_scr[...] = h.astype(BF16)

    def project(c0, width):
        return jnp.dot(h_scr[...], w_ref[:, c0:c0 + width], preferred_element_type=F32)

    if latent:
        for c0 in range(0, ATTN_W, PROJ_TN):
            acc = project(c0, PROJ_TN)
            for hh in range(PROJ_TN // HEAD_DIM):
                sl = slice(hh * HEAD_DIM, (hh + 1) * HEAD_DIM)
                dst = slice(c0 + hh * HEAD_DIM, c0 + (hh + 1) * HEAD_DIM)
                q_ref[0, :, dst] = _head_norm_rope(acc[:, sl], qg_ref[...], cos, sin).astype(BF16)

    acc = project(ATTN_W, 2 * KV_W)
    for hh in range(N_KV_HEADS):
        sl = slice(hh * HEAD_DIM, (hh + 1) * HEAD_DIM)
        k_ref[0, :, sl] = _head_norm_rope(acc[:, sl], kg_ref[...], cos, sin).astype(BF16)
    v_ref[0] = acc[:, KV_W:2 * KV_W].astype(BF16)

    rnn0 = ATTN_W + 2 * KV_W
    for c0 in range(0, D_RNN, PROJ_TN):
        xr_ref[0, :, c0:c0 + PROJ_TN] = project(rnn0 + c0, PROJ_TN)
    if latent:
        for c0 in range(0, D_RNN, PROJ_TN):
            gr_ref[0, :, c0:c0 + PROJ_TN] = project(rnn0 + D_RNN + c0, PROJ_TN)


def _in_proj(x, shift, scale, g, w, qg, kg, cos, sin, *, latent):
    bn, length, d = x.shape
    tm = min(PROJ_TM, length)
    per_batch = shift.shape[0] > 1
    mod_map = (lambda b, i: (b, 0, 0)) if per_batch else (lambda b, i: (0, 0, 0))
    vec_map = lambda b, i: (0, 0)
    row = lambda b, i: (b, i, 0)

    in_specs = [
        pl.BlockSpec((1, tm, d), row),
        pl.BlockSpec((1, 1, d), mod_map),
        pl.BlockSpec((1, 1, d), mod_map),
        pl.BlockSpec((1, d), vec_map),
        pl.BlockSpec(w.shape, vec_map),
    ]
    args = [x, shift, scale, g, w]
    k_spec = pl.BlockSpec((1, tm, KV_W), row)
    xr_spec = pl.BlockSpec((1, tm, D_RNN), row)
    k_shape = jax.ShapeDtypeStruct((bn, length, KV_W), BF16)
    xr_shape = jax.ShapeDtypeStruct((bn, length, D_RNN), F32)
    if latent:
        in_specs += [pl.BlockSpec((1, HEAD_DIM), vec_map), pl.BlockSpec((1, HEAD_DIM), vec_map),
                     pl.BlockSpec((tm, HEAD_DIM), lambda b, i: (i, 0)),
                     pl.BlockSpec((tm, HEAD_DIM), lambda b, i: (i, 0))]
        args += [qg, kg, cos, sin]
        out_specs = [pl.BlockSpec((1, tm, ATTN_W), row), k_spec, k_spec, xr_spec, xr_spec]
        out_shape = [jax.ShapeDtypeStruct((bn, length, ATTN_W), BF16), k_shape, k_shape, xr_shape, xr_shape]
    else:
        in_specs += [pl.BlockSpec((1, HEAD_DIM), vec_map)]
        args += [kg]
        out_specs = [k_spec, k_spec, xr_spec]
        out_shape = [k_shape, k_shape, xr_shape]

    return pl.pallas_call(
        functools.partial(_in_proj_kernel, latent=latent),
        grid=(bn, length // tm),
        in_specs=in_specs,
        out_specs=out_specs,
        out_shape=out_shape,
        scratch_shapes=[pltpu.VMEM((tm, d), BF16)],
        compiler_params=_params(),
        name="in_proj_latent" if latent else "in_proj_context",
    )(*args)


def _dwconv(x, w, b):
    n = x.shape[0]
    row = lax.broadcasted_iota(jnp.int32, x.shape, 0)
    y = b + x * w[2:3]
    y = y + jnp.where(row >= 2, pltpu.roll(x, 2, 0), 0.0) * w[0:1]
    y = y + jnp.where(row >= 1, pltpu.roll(x, 1, 0), 0.0) * w[1:2]
    y = y + jnp.where(row < n - 1, pltpu.roll(x, n - 1, 0), 0.0) * w[3:4]
    return y


def _scan_chunk(a, b, carry, reverse):
    tc = a.shape[0]
    nv = tc // SUBLANES
    a3 = a.reshape(nv, SUBLANES, LANES)
    b3 = b.reshape(nv, SUBLANES, LANES)
    row = lax.broadcasted_iota(jnp.int32, a3.shape, 1)
    step = 1
    while step < SUBLANES:
        if reverse:
            shift, valid = SUBLANES - step, row < SUBLANES - step
        else:
            shift, valid = step, row >= step
        a_sh = pltpu.roll(a3, shift, 1)
        b_sh = pltpu.roll(b3, shift, 1)
        b3 = jnp.where(valid, a3 * b_sh + b3, b3)
        a3 = jnp.where(valid, a3 * a_sh, a3)
        step *= 2
    last = 0 if reverse else SUBLANES - 1
    a_last = jnp.broadcast_to(a3[:, last:last + 1, :], a3.shape)
    b_last = jnp.broadcast_to(b3[:, last:last + 1, :], b3.shape)
    hs = [None] * nv
    for v in (range(nv - 1, -1, -1) if reverse else range(nv)):
        hs[v] = b3[v] + a3[v] * carry
        carry = b_last[v] + a_last[v] * carry
    return hs, carry


def _rglru_kernel(xl_ref, xc_ref, gl_ref, cw_ref, cb_ref, w_ref, bias_ref, lam_ref, o_ref,
                  xpad, xconv_c, xcs, hloc_f, aloc_f, hloc_b, aloc_b, hsum):
    cw = cw_ref[...]
    cb = cb_ref[...]
    length = xl_ref.shape[1]
    tseg = length // SUBLANES
    pitch = tseg + SEG_PAD
    n_j = tseg // SCAN_TJ
    n_ctx = xc_ref.shape[1] // SCAN_TC

    xconv_c[...] = _dwconv(xc_ref[0], cw, cb)

    pad_zeros = jnp.zeros((SEG_PAD, LANES), F32)
    xpad[0:SEG_PAD, :] = pad_zeros
    for s in range(SUBLANES):
        base = SEG_PAD + s * pitch
        xpad[base:base + tseg, :] = xl_ref[0, s * tseg:(s + 1) * tseg, :]
        pad = base + tseg
        xpad[pad:pad + SEG_PAD, :] = pad_zeros
        if s + 1 < SUBLANES:
            xpad[pad:pad + 1, :] = xl_ref[0, (s + 1) * tseg:(s + 1) * tseg + 1, :]
        xpad[pad + SEG_PAD - 2:pad + SEG_PAD, :] = xl_ref[0, (s + 1) * tseg - 2:(s + 1) * tseg, :]

    taps = [jnp.broadcast_to(cw[t:t + 1], (SUBLANES, LANES)) for t in range(CONV_W)]
    bias_rows = jnp.broadcast_to(cb, (SUBLANES, LANES))

    def conv_step(kk, _):
        j0 = kk * SCAN_TJ
        xs = [xpad[pl.ds(SEG_PAD + j0 - CONV_W // 2 + t, SUBLANES, stride=pitch), :]
              for t in range(SCAN_TJ + CONV_W - 1)]
        for jj in range(SCAN_TJ):
            y = bias_rows
            for t in range(CONV_W):
                y = y + xs[jj + t] * taps[t]
            xcs[pl.ds(pl.multiple_of((j0 + jj) * SUBLANES, SUBLANES), SUBLANES), :] = y
        return 0

    lax.fori_loop(0, n_j, conv_step, 0)

    def gates(xc, d):
        z = jnp.dot(xc.astype(BF16), w_ref[d, 0], preferred_element_type=F32) + bias_ref[d, 0]
        t_r = jnp.tanh(z[:, :RNN_BLOCK_W])
        t_i = jnp.tanh(z[:, RNN_BLOCK_W:])
        lam = lam_ref[d, 0]
        softplus_neg_lam = jnp.maximum(-lam, 0.0) + jnp.log1p(jnp.exp(-jnp.abs(lam)))
        half_rate = (-0.5 * RG_C) * softplus_neg_lam
        log_a = half_rate * t_r + half_rate
        a = jnp.exp(log_a)
        th = jnp.tanh(log_a)
        b = (jnp.sqrt(-2.0 * th / (1.0 - th)) * (0.5 * xc)) * (t_i + 1.0)
        return a, b

    ctx_state = []
    for d, reverse in ((0, False), (1, True)):
        def ctx_step(kk, carry, d=d, reverse=reverse):
            k = (n_ctx - 1 - kk) if reverse else kk
            r0 = pl.multiple_of(k * SCAN_TC, SCAN_TC)
            a, b = gates(xconv_c[pl.ds(r0, SCAN_TC), :], d)
            _, carry = _scan_chunk(a, b, carry, reverse)
            return carry

        ctx_state.append(lax.fori_loop(0, n_ctx, ctx_step, jnp.zeros((SUBLANES, LANES), F32)))

    def local_step(kk, carry):
        new = []
        for d, (h, acc_a) in enumerate(carry):
            reverse = d == 1
            j0 = ((n_j - 1 - kk) if reverse else kk) * SCAN_TJ
            src = pl.ds(pl.multiple_of(j0 * SUBLANES, SCAN_TJ * SUBLANES), SCAN_TJ * SUBLANES)
            a, b = gates(xcs[src, :], d)
            hloc, aloc = (hloc_b, aloc_b) if reverse else (hloc_f, aloc_f)
            for jj in (range(SCAN_TJ - 1, -1, -1) if reverse else range(SCAN_TJ)):
                rows = slice(jj * SUBLANES, (jj + 1) * SUBLANES)
                h = a[rows] * h + b[rows]
                acc_a = a[rows] * acc_a
                dst = pl.ds(pl.multiple_of((j0 + jj) * SUBLANES, SUBLANES), SUBLANES)
                hloc[dst, :] = h
                aloc[dst, :] = acc_a
            new.append((h, acc_a))
        return tuple(new)

    zeros = jnp.zeros((SUBLANES, LANES), F32)
    ones = jnp.ones((SUBLANES, LANES), F32)
    (h_f, a_f), (h_b, a_b) = lax.fori_loop(0, n_j, local_step, ((zeros, ones), (zeros, ones)))

    def entry_states(h_end, a_end, c_in, reverse):
        row = lax.broadcasted_iota(jnp.int32, (SUBLANES, LANES), 0)
        first = SUBLANES - 1 if reverse else 0
        c = c_in[0:1]
        out = jnp.where(row == first, jnp.broadcast_to(c, (SUBLANES, LANES)), 0.0)
        for k in range(1, SUBLANES):
            s = first - k if reverse else first + k
            prev = s + 1 if reverse else s - 1
            c = h_end[prev:prev + 1] + a_end[prev:prev + 1] * c
            out = jnp.where(row == s, jnp.broadcast_to(c, (SUBLANES, LANES)), out)
        return out

    c_f = entry_states(h_f, a_f, ctx_state[0], False)
    c_b = entry_states(h_b, a_b, ctx_state[1], True)

    def fix_step(kk, _):
        j0 = kk * SCAN_TJ
        for jj in range(SCAN_TJ):
            src = pl.ds(pl.multiple_of((j0 + jj) * SUBLANES, SUBLANES), SUBLANES)
            h = (hloc_f[src, :] + aloc_f[src, :] * c_f) + (hloc_b[src, :] + aloc_b[src, :] * c_b)
            hsum[pl.ds(j0 + jj, SUBLANES, stride=pitch), :] = h
        return 0

    lax.fori_loop(0, n_j, fix_step, 0)

    for s in range(SUBLANES):
        for r0 in range(0, tseg, SCAN_TC):
            src = slice(s * pitch + r0, s * pitch + r0 + SCAN_TC)
            dst = slice(s * tseg + r0, s * tseg + r0 + SCAN_TC)
            o_ref[0, dst, :] = (hsum[src, :] * _gelu(gl_ref[0, dst, :])).astype(BF16)


def _rglru(xl, xc, gl, conv_w, conv_b, w_gate, b_gate, lam):
    bn, s, _ = xl.shape
    ctx_len = xc.shape[1]
    bw = RNN_BLOCK_W
    padded = s + SUBLANES * SEG_PAD
    col = lambda b, n: (b, 0, n)
    return pl.pallas_call(
        _rglru_kernel,
        grid=(bn, RNN_BLOCKS),
        in_specs=[
            pl.BlockSpec((1, s, bw), col),
            pl.BlockSpec((1, ctx_len, bw), col),
            pl.BlockSpec((1, s, bw), col),
            pl.BlockSpec((CONV_W, bw), lambda b, n: (0, n)),
            pl.BlockSpec((1, bw), lambda b, n: (0, n)),
            pl.BlockSpec((2, 1, bw, 2 * bw), lambda b, n: (0, n, 0, 0)),
            pl.BlockSpec((2, 1, 1, 2 * bw), lambda b, n: (0, n, 0, 0)),
            pl.BlockSpec((2, 1, 1, bw), lambda b, n: (0, n, 0, 0)),
        ],
        out_specs=pl.BlockSpec((1, s, bw), col),
        out_shape=jax.ShapeDtypeStruct((bn, s, D_RNN), BF16),
        scratch_shapes=[
            pltpu.VMEM((SEG_PAD + padded, bw), F32),
            pltpu.VMEM((ctx_len, bw), F32),
            pltpu.VMEM((s, bw), F32),
            pltpu.VMEM((s, bw), F32), pltpu.VMEM((s, bw), F32),
            pltpu.VMEM((s, bw), F32), pltpu.VMEM((s, bw), F32),
            pltpu.VMEM((padded, bw), F32),
        ],
        compiler_params=_params(),
        name="rglru_bidirectional",
    )(xl, xc, gl, conv_w, conv_b, w_gate, b_gate, lam)


def _attn_kernel(q_ref, kc_ref, kl_ref, vc_ref, vl_ref, o_ref, k_scr, v_scr):
    n_ctx = kc_ref.shape[1]

    @pl.when(pl.program_id(2) == 0)
    def _():
        k_scr[0:n_ctx, :] = kc_ref[0]
        k_scr[n_ctx:, :] = kl_ref[0]
        v_scr[0:n_ctx, 0:HEAD_DIM] = vc_ref[0]
        v_scr[n_ctx:, 0:HEAD_DIM] = vl_ref[0]
        v_scr[:, HEAD_DIM:] = jnp.ones((v_scr.shape[0], HEAD_DIM), BF16)

    k = k_scr[...]
    v = v_scr[...]
    c = (HEAD_DIM ** -0.5) * 1.4426950408889634
    for r0 in range(0, q_ref.shape[1], ATTN_SUB):
        rows = slice(r0, r0 + ATTN_SUB)
        for hh in range(N_HEADS // N_KV_HEADS):
            sl = slice(hh * HEAD_DIM, (hh + 1) * HEAD_DIM)
            s = lax.dot_general(q_ref[0, rows, sl], k, (((1,), (1,)), ((), ())), preferred_element_type=F32)
            m = jnp.max(s, axis=-1, keepdims=True)
            p = jnp.exp2((s - m) * c).astype(BF16)
            ov = jnp.dot(p, v, preferred_element_type=F32)
            o_ref[0, rows, sl] = (ov[:, :HEAD_DIM] / ov[:, HEAD_DIM:HEAD_DIM + 1]).astype(BF16)


def _attention(q, kc, kl, vc, vl):
    bn, s, _ = q.shape
    n_ctx, n_lat = kc.shape[1], kl.shape[1]
    t = n_ctx + n_lat
    gw = ATTN_W // N_KV_HEADS
    kv_spec = lambda n: pl.BlockSpec((1, n, HEAD_DIM), lambda b, h, i: (b, 0, h))
    return pl.pallas_call(
        _attn_kernel,
        grid=(bn, N_KV_HEADS, s // ATTN_TQ),
        in_specs=[
            pl.BlockSpec((1, ATTN_TQ, gw), lambda b, h, i: (b, i, h)),
            kv_spec(n_ctx), kv_spec(n_lat), kv_spec(n_ctx), kv_spec(n_lat),
        ],
        out_specs=pl.BlockSpec((1, ATTN_TQ, gw), lambda b, h, i: (b, i, h)),
        out_shape=jax.ShapeDtypeStruct((bn, s, ATTN_W), BF16),
        scratch_shapes=[pltpu.VMEM((t, HEAD_DIM), BF16), pltpu.VMEM((t, 2 * HEAD_DIM), BF16)],
        compiler_params=_params(),
        name="gqa_attention",
    )(q, kc, kl, vc, vl)


def _residual_and_next(x, mix, gate, g_post, g_next, shift_next, scale_next):
    x1 = x + _rms(mix, gate * g_post)
    h = _rms(x1, g_next * (1.0 + scale_next)) + shift_next
    return x1, h.astype(BF16)


def _out_proj_kernel(a1_ref, a2_ref, w_ref, x_ref, gate_ref, g1_ref, g2_ref, sh_ref, sc_ref, x1_ref, h_ref):
    ka = a1_ref.shape[2]
    for r0 in range(0, x_ref.shape[1], EPI_SUB):
        rows = slice(r0, r0 + EPI_SUB)
        mix = jnp.dot(a1_ref[0, rows, :], w_ref[0:ka, :], preferred_element_type=F32)
        mix = mix + jnp.dot(a2_ref[0, rows, :], w_ref[ka:, :], preferred_element_type=F32)
        x1, h = _residual_and_next(x_ref[0, rows, :], mix, gate_ref[0], g1_ref[...], g2_ref[...],
                                   sh_ref[0], sc_ref[0])
        x1_ref[0, rows, :] = x1
        h_ref[0, rows, :] = h


def _out_proj(a1, a2, w, x, gate, g1, g2, shift, scale):
    bn, s, d = x.shape
    tm = OUT_TM
    row = lambda b, i: (b, i, 0)
    mod = lambda b, i: (b, 0, 0)
    vec = lambda b, i: (0, 0)
    return pl.pallas_call(
        _out_proj_kernel,
        grid=(bn, s // tm),
        in_specs=[
            pl.BlockSpec((1, tm, a1.shape[2]), row),
            pl.BlockSpec((1, tm, a2.shape[2]), row),
            pl.BlockSpec(w.shape, vec, pipeline_mode=pl.Buffered(1)),
            pl.BlockSpec((1, tm, d), row),
            pl.BlockSpec((1, 1, d), mod),
            pl.BlockSpec((1, d), vec),
            pl.BlockSpec((1, d), vec),
            pl.BlockSpec((1, 1, d), mod),
            pl.BlockSpec((1, 1, d), mod),
        ],
        out_specs=[pl.BlockSpec((1, tm, d), row), pl.BlockSpec((1, tm, d), row)],
        out_shape=[jax.ShapeDtypeStruct((bn, s, d), F32), jax.ShapeDtypeStruct((bn, s, d), BF16)],
        compiler_params=pltpu.CompilerParams(vmem_limit_bytes=BIG_VMEM_LIMIT_BYTES),
        name="out_proj_residual",
    )(a1, a2, w, x, gate, g1, g2, shift, scale)


def _ffn_kernel(*refs, has_next, na):
    if has_next:
        (h_ref, w1_ref, w2_ref, x_ref, gate_ref, g3_ref, gn_ref, sh_ref, sc_ref,
         x2_ref, hn_ref, t_scr) = refs
    else:
        h_ref, w1_ref, w2_ref, x_ref, gate_ref, g3_ref, x2_ref, t_scr = refs
    j = pl.program_id(2)
    ta = t_scr.shape[2]
    tb = w2_ref.shape[1]

    @pl.when(j < na)
    def _():
        t = jnp.dot(h_ref[0], w1_ref[...], preferred_element_type=F32)
        t = jnp.maximum(t, 0.0)
        t_scr[j] = (t * t).astype(BF16)

    last = pl.num_programs(2) - 1

    def down_proj(rows):
        y = jnp.dot(t_scr[0, rows, :], w2_ref[0:ta, :], preferred_element_type=F32)
        for a in range(1, na):
            y = y + jnp.dot(t_scr[a, rows, :], w2_ref[a * ta:(a + 1) * ta, :], preferred_element_type=F32)
        return y

    @pl.when(jnp.logical_and(j >= na, j < last))
    def _():
        col = pl.multiple_of((j - na) * tb, tb)
        x2_ref[0, :, pl.ds(col, tb)] = down_proj(slice(None))

    @pl.when(j == last)
    def _():
        col0 = x2_ref.shape[2] - tb
        for r0 in range(0, x2_ref.shape[1], EPI_SUB):
            rows = slice(r0, r0 + EPI_SUB)
            x2_ref[0, rows, col0:] = down_proj(rows)
            if has_next:
                x2, hn = _residual_and_next(x_ref[0, rows, :], x2_ref[0, rows, :], gate_ref[0], g3_ref[...],
                                            gn_ref[...], sh_ref[0], sc_ref[0])
                x2_ref[0, rows, :] = x2
                hn_ref[0, rows, :] = hn
            else:
                x2_ref[0, rows, :] = x_ref[0, rows, :] + _rms(x2_ref[0, rows, :], gate_ref[0] * g3_ref[...])


def _ffn(h, w1, w2, layer, x, gate, g3, nxt=None):
    bn, s, d = x.shape
    tm = FFN_TM
    na, ta = w1.shape[1], w1.shape[3]
    nb, tb = w2.shape[1], w2.shape[3]
    dff = na * ta
    row = lambda b, i, j: (b, i, 0)
    mod = lambda b, i, j: (b, 0, 0)
    vec = lambda b, i, j: (0, 0)
    in_specs = [
        pl.BlockSpec((1, tm, d), row),
        pl.BlockSpec((None, None, d, ta), lambda b, i, j: (layer, jnp.minimum(j, na - 1), 0, 0)),
        pl.BlockSpec((None, None, dff, tb), lambda b, i, j: (layer, jnp.maximum(j - na, 0), 0, 0)),
        pl.BlockSpec((1, tm, d), row),
        pl.BlockSpec((1, 1, d), mod),
        pl.BlockSpec((1, d), vec),
    ]
    args = [h, w1, w2, x, gate, g3]
    out_specs = [pl.BlockSpec((1, tm, d), row)]
    out_shape = [jax.ShapeDtypeStruct((bn, s, d), F32)]
    if nxt is not None:
        in_specs += [pl.BlockSpec((1, d), vec), pl.BlockSpec((1, 1, d), mod), pl.BlockSpec((1, 1, d), mod)]
        args += list(nxt)
        out_specs.append(pl.BlockSpec((1, tm, d), row))
        out_shape.append(jax.ShapeDtypeStruct((bn, s, d), BF16))
    return pl.pallas_call(
        functools.partial(_ffn_kernel, has_next=nxt is not None, na=na),
        grid=(bn, s // tm, na + nb),
        in_specs=in_specs,
        out_specs=out_specs,
        out_shape=out_shape,
        scratch_shapes=[pltpu.VMEM((na, tm, ta), BF16)],
        compiler_params=pltpu.CompilerParams(vmem_limit_bytes=BIG_VMEM_LIMIT_BYTES),
        name="sq_relu_mlp",
    )(*args)


def _gm_in_kernel(h_ref, w_ref, b_ref, z_ref):
    c0 = 0
    for width in GM_GROUP_COLS:
        cols = slice(c0, c0 + width)
        z = jnp.dot(h_ref[0], w_ref[:, cols], preferred_element_type=F32) + b_ref[:, cols]
        z_ref[0, :, cols] = _gelu(z)
        c0 += width


def _gm_in(h, w, b):
    bn, s, d = h.shape
    n = w.shape[1]
    return pl.pallas_call(
        _gm_in_kernel,
        grid=(n // GM_TN, bn, s // GM_TM),
        in_specs=[
            pl.BlockSpec((1, GM_TM, d), lambda j, b_, i: (b_, i, 0)),
            pl.BlockSpec((d, GM_TN), lambda j, b_, i: (0, j)),
            pl.BlockSpec((1, GM_TN), lambda j, b_, i: (0, j)),
        ],
        out_specs=pl.BlockSpec((1, GM_TM, GM_TN), lambda j, b_, i: (b_, i, j)),
        out_shape=jax.ShapeDtypeStruct((bn, s, n), F32),
        compiler_params=_params(),
        name="gmlp_in_proj",
    )(h, w, b)


def _gm_out_kernel(u_ref, v_ref, vg_ref, vb_ref, wsp_ref, bsp_ref, w_ref, x_ref, gate_ref, g1_ref, g2_ref,
                   sh_ref, sc_ref, x1_ref, h_ref, vn_scr, gated_scr):
    for r0 in range(0, x_ref.shape[1], EPI_SUB):
        sub = slice(r0, r0 + EPI_SUB)
        v = v_ref[0, sub, :]
        mu = jnp.mean(v, axis=-1, keepdims=True)
        vc = v - mu
        var = jnp.mean(vc * vc, axis=-1, keepdims=True)
        vn_scr[sub, :] = (vc * lax.rsqrt(var + EPS) * vg_ref[...] + vb_ref[...]).astype(BF16)
        for c0 in range(r0, r0 + EPI_SUB, CHUNK):
            rows = slice(c0, c0 + CHUNK)
            for g in range(GM_GROUPS):
                cols = slice(g * GM_GROUP_W, (g + 1) * GM_GROUP_W)
                sv = jnp.dot(wsp_ref[g], vn_scr[rows, cols], preferred_element_type=F32) + bsp_ref[:, g:g + 1]
                gated_scr[rows, cols] = (u_ref[0, rows, cols] * sv).astype(BF16)
        mix = jnp.dot(gated_scr[sub, :], w_ref[...], preferred_element_type=F32)
        x1, h = _residual_and_next(x_ref[0, sub, :], mix, gate_ref[0], g1_ref[...], g2_ref[...],
                                   sh_ref[0], sc_ref[0])
        x1_ref[0, sub, :] = x1
        h_ref[0, sub, :] = h


def _gm_out(z, vg, vb, wsp, bsp_t, w, x, gate, g1, g2, shift, scale):
    bn, s, d = x.shape
    dg = z.shape[2] // 2
    tm = GMO_TM
    row = lambda b, i: (b, i, 0)
    mod = lambda b, i: (b, 0, 0)
    vec = lambda b, i: (0, 0)
    return pl.pallas_call(
        _gm_out_kernel,
        grid=(bn, s // tm),
        in_specs=[
            pl.BlockSpec((1, tm, dg), lambda b, i: (b, i, 0)),
            pl.BlockSpec((1, tm, dg), lambda b, i: (b, i, 1)),
            pl.BlockSpec((1, dg), vec),
            pl.BlockSpec((1, dg), vec),
            pl.BlockSpec(wsp.shape, lambda b, i: (0, 0, 0), pipeline_mode=pl.Buffered(1)),
            pl.BlockSpec(bsp_t.shape, vec),
            pl.BlockSpec(w.shape, vec, pipeline_mode=pl.Buffered(1)),
            pl.BlockSpec((1, tm, d), row),
            pl.BlockSpec((1, 1, d), mod),
            pl.BlockSpec((1, d), vec),
            pl.BlockSpec((1, d), vec),
            pl.BlockSpec((1, 1, d), mod),
            pl.BlockSpec((1, 1, d), mod),
        ],
        out_specs=[pl.BlockSpec((1, tm, d), row), pl.BlockSpec((1, tm, d), row)],
        out_shape=[jax.ShapeDtypeStruct((bn, s, d), F32), jax.ShapeDtypeStruct((bn, s, d), BF16)],
        scratch_shapes=[pltpu.VMEM((tm, dg), BF16), pltpu.VMEM((tm, dg), BF16)],
        compiler_params=_params(),
        name="gmlp_spatial_out_proj",
    )(z, z, vg, vb, wsp, bsp_t, w, x, gate, g1, g2, shift, scale)


def _rope_tables(n):
    t = np.arange(n)
    r_idx = (t // GRID_W).astype(np.float32)
    c_idx = (t % GRID_W).astype(np.float32)
    freqs = np.float32(ROPE_THETA) ** (-np.arange(ROPE_PAIRS, dtype=np.float32) / np.float32(ROPE_PAIRS))
    ang_r = r_idx[:, None] * freqs
    ang_c = c_idx[:, None] * freqs
    cos = np.concatenate([np.cos(ang_r), np.cos(ang_r), np.cos(ang_c), np.cos(ang_c)], axis=-1)
    sin = np.concatenate([-np.sin(ang_r), np.sin(ang_r), -np.sin(ang_c), np.sin(ang_c)], axis=-1)
    return jnp.asarray(cos, F32), jnp.asarray(sin, F32)


def kernel(x, c, ctx, c_ctx, w_mod, b_mod, norm_g, w_ff_in, w_ff_out, ar_w_in, ar_q_g, ar_k_g, ar_conv_w,
           ar_conv_b, ar_wa, ar_ba, ar_wx, ar_bx, ar_lambda, ar_w_out, gm_w_in, gm_b_in, gm_v_g, gm_v_b,
           gm_w_sp, gm_b_sp, gm_w_out):
    bn, s, d = x.shape
    depth = w_mod.shape[0]
    assert depth == 2, "layer pattern implemented for one attention/recurrent layer followed by one gMLP layer"

    rows = -(-(bn + 1) // SUBLANES) * SUBLANES
    cc = jnp.concatenate([c, c_ctx[None, :], jnp.zeros((rows - bn - 1, d), F32)], axis=0)
    mods = _modulation(cc, w_mod, b_mod)

    def lat_mod(layer, k):
        return mods[layer, :bn, k * d:(k + 1) * d].reshape(bn, 1, d)

    def ctx_mod(layer, k):
        return mods[layer, bn:bn + 1, k * d:(k + 1) * d].reshape(1, 1, d)

    g = norm_g.reshape(depth, 4, 1, d)
    w_ff1 = _to_bf16_col_blocks(w_ff_in, FFN_TA)
    w_ff2 = _to_bf16_col_blocks(w_ff_out, FFN_TB)

    w_in = ar_w_in[0].astype(BF16)
    qg = ar_q_g[0].reshape(1, HEAD_DIM)
    kg = ar_k_g[0].reshape(1, HEAD_DIM)
    cos, sin = _rope_tables(s)
    ql, kl, vl, xrl, grl = _in_proj(x, lat_mod(0, 0), lat_mod(0, 1), g[0, 0], w_in, qg, kg, cos, sin, latent=True)
    kc, vc, xrc = _in_proj(ctx, ctx_mod(0, 0), ctx_mod(0, 1), g[0, 0], w_in, qg, kg, None, None, latent=False)

    attn = _attention(ql, kc, kl, vc, vl)

    w_gate = (0.5 * jnp.concatenate([ar_wa[0], ar_wx[0]], axis=-1)).astype(BF16)
    b_gate = 0.5 * jnp.concatenate([ar_ba[0].reshape(2, RNN_BLOCKS, 1, RNN_BLOCK_W),
                                    ar_bx[0].reshape(2, RNN_BLOCKS, 1, RNN_BLOCK_W)], axis=-1)
    lam = ar_lambda[0].reshape(2, RNN_BLOCKS, 1, RNN_BLOCK_W)
    rnn = _rglru(xrl, xrc, grl, ar_conv_w[0], ar_conv_b[0].reshape(1, D_RNN), w_gate, b_gate, lam)

    x1, h = _out_proj(attn, rnn, ar_w_out[0].astype(BF16), x, lat_mod(0, 2), g[0, 1], g[0, 2],
                      lat_mod(0, 3), lat_mod(0, 4))
    x2, h = _ffn(h, w_ff1, w_ff2, 0, x1, lat_mod(0, 5), g[0, 3], nxt=(g[1, 0], lat_mod(1, 0), lat_mod(1, 1)))

    z = _gm_in(h, gm_w_in[0].astype(BF16), gm_b_in[0].reshape(1, -1))
    x3, h = _gm_out(z, gm_v_g[0].reshape(1, -1), gm_v_b[0].reshape(1, -1), gm_w_sp[0].astype(BF16),
                    gm_b_sp[0].T, gm_w_out[0].astype(BF16), x2, lat_mod(1, 2), g[1, 1], g[1, 2],
                    lat_mod(1, 3), lat_mod(1, 4))
    (x4,) = _ffn(h, w_ff1, w_ff2, 1, x3, lat_mod(1, 5), g[1, 3])
    return x4
```

```python
import functools

import jax
import jax.numpy as jnp
import numpy as np
from jax import lax
from jax.experimental import pallas as pl
from jax.experimental.pallas import tpu as pltpu

F32 = jnp.float32
BF16 = jnp.bfloat16

GRID_W = 64
N_HEADS = 8
N_KV_HEADS = 2
HEAD_DIM = 128
ATTN_W = N_HEADS * HEAD_DIM
KV_W = N_KV_HEADS * HEAD_DIM
ROPE_THETA = 10000.0
ROPE_PAIRS = HEAD_DIM // 4
D_RNN = 1024
RNN_BLOCKS = 8
RNN_BLOCK_W = D_RNN // RNN_BLOCKS
CONV_W = 4
RG_C = 8.0
GM_GROUPS = 16
GM_GROUP_W = 128
CHUNK = 128
EPS = 1e-6

LANES = 128
SUBLANES = 8
VMEM_LIMIT_BYTES = 56 * 1024 * 1024
BIG_VMEM_LIMIT_BYTES = 63 * 1024 * 1024

MOD_TN = 1024
PROJ_TN = 512
PROJ_TM = 512
SCAN_TC = 256
SCAN_TJ = 32
SEG_PAD = 8
ATTN_TQ = 1024
ATTN_SUB = 256
OUT_TM = 512
EPI_SUB = 256
FFN_TM = 512
FFN_TA = 1024
FFN_TB = 512
GM_TM = 1024
GM_TN = 2048
GM_GROUP_COLS = (1024, 1024)
GMO_TM = 512
CAST_BLOCK_BYTES = 8 * 1024 * 1024


def _params():
    return pltpu.CompilerParams(vmem_limit_bytes=VMEM_LIMIT_BYTES)


def _rms(x, g):
    ms = jnp.mean(x * x, axis=-1, keepdims=True)
    return x * lax.rsqrt(ms + EPS) * g


def _sigmoid(x):
    return 1.0 / (1.0 + jnp.exp(-x))


def _gelu(x):
    c = 0.7978845608028654
    return 0.5 * x * (1.0 + jnp.tanh(c * (x + 0.044715 * (x * x * x))))


def _mod_kernel(cc_ref, w_ref, b_ref, o_ref):
    c = cc_ref[...]
    s = c * _sigmoid(c)
    o_ref[0] = jnp.dot(s.astype(BF16), w_ref[0].astype(BF16), preferred_element_type=F32) + b_ref[0]


def _modulation(cc, w_mod, b_mod):
    depth, d, n = w_mod.shape
    rows = cc.shape[0]
    return pl.pallas_call(
        _mod_kernel,
        grid=(depth, n // MOD_TN),
        in_specs=[
            pl.BlockSpec((rows, d), lambda l, j: (0, 0)),
            pl.BlockSpec((1, d, MOD_TN), lambda l, j: (l, 0, j)),
            pl.BlockSpec((1, 1, MOD_TN), lambda l, j: (l, 0, j)),
        ],
        out_specs=pl.BlockSpec((1, rows, MOD_TN), lambda l, j: (l, 0, j)),
        out_shape=jax.ShapeDtypeStruct((depth, rows, n), F32),
        compiler_params=_params(),
        name="adaln_modulation",
    )(cc, w_mod, b_mod.reshape(depth, 1, n))


def _cast_kernel(w_ref, o_ref):
    tc = o_ref.shape[3]
    for a in range(o_ref.shape[1]):
        o_ref[0, a] = w_ref[0, :, a * tc:(a + 1) * tc].astype(BF16)


def _to_bf16_col_blocks(w, tc):
    layers, r, c = w.shape
    tr = min(r, CAST_BLOCK_BYTES // (c * 4))
    return pl.pallas_call(
        _cast_kernel,
        grid=(layers, r // tr),
        in_specs=[pl.BlockSpec((1, tr, c), lambda l, i: (l, i, 0))],
        out_specs=pl.BlockSpec((1, c // tc, tr, tc), lambda l, i: (l, 0, i, 0)),
        out_shape=jax.ShapeDtypeStruct((layers, c // tc, r, tc), BF16),
        compiler_params=_params(),
        name="weights_to_bf16",
    )(w)


def _head_norm_rope(xh, gain, cos, sin):
    y = _rms(xh, gain)
    if cos is None:
        return y
    lane = lax.broadcasted_iota(jnp.int32, y.shape, 1)
    first_half = (lane % (2 * ROPE_PAIRS)) < ROPE_PAIRS
    partner = jnp.where(first_half,
                        pltpu.roll(y, HEAD_DIM - ROPE_PAIRS, 1),
                        pltpu.roll(y, ROPE_PAIRS, 1))
    return y * cos + partner * sin


def _in_proj_kernel(*refs, latent):
    if latent:
        (x_ref, sh_ref, sc_ref, g_ref, w_ref, qg_ref, kg_ref, cos_ref, sin_ref,
         q_ref, k_ref, v_ref, xr_ref, gr_ref, h_scr) = refs
        cos, sin = cos_ref[...], sin_ref[...]
    else:
        (x_ref, sh_ref, sc_ref, g_ref, w_ref, kg_ref, k_ref, v_ref, xr_ref, h_scr) = refs
        cos = sin = None
    h = _rms(x_ref[0], g_ref[...] * (1.0 + sc_ref[0])) + sh_ref[0]
    h_scr[...] = h.astype(BF16)

    def project(c0, width):
        return jnp.dot(h_scr[...], w_ref[:, c0:c0 + width], preferred_element_type=F32)

    if latent:
        for c0 in range(0, ATTN_W, PROJ_TN):
            acc = project(c0, PROJ_TN)
            for hh in range(PROJ_TN // HEAD_DIM):
                sl = slice(hh * HEAD_DIM, (hh + 1) * HEAD_DIM)
                dst = slice(c0 + hh * HEAD_DIM, c0 + (hh + 1) * HEAD_DIM)
                q_ref[0, :, dst] = _head_norm_rope(acc[:, sl], qg_ref[...], cos, sin).astype(BF16)

    acc = project(ATTN_W, 2 * KV_W)
    for hh in range(N_KV_HEADS):
        sl = slice(hh * HEAD_DIM, (hh + 1) * HEAD_DIM)
        k_ref[0, :, sl] = _head_norm_rope(acc[:, sl], kg_ref[...], cos, sin).astype(BF16)
    v_ref[0] = acc[:, KV_W:2 * KV_W].astype(BF16)

    rnn0 = ATTN_W + 2 * KV_W
    for c0 in range(0, D_RNN, PROJ_TN):
        xr_ref[0, :, c0:c0 + PROJ_TN] = project(rnn0 + c0, PROJ_TN)
    if latent:
        for c0 in range(0, D_RNN, PROJ_TN):
            gr_ref[0, :, c0:c0 + PROJ_TN] = project(rnn0 + D_RNN + c0, PROJ_TN)


def _in_proj(x, shift, scale, g, w, qg, kg, cos, sin, *, latent):
    bn, length, d = x.shape
    tm = min(PROJ_TM, length)
    per_batch = shift.shape[0] > 1
    mod_map = (lambda b, i: (b, 0, 0)) if per_batch else (lambda b, i: (0, 0, 0))
    vec_map = lambda b, i: (0, 0)
    row = lambda b, i: (b, i, 0)

    in_specs = [
        pl.BlockSpec((1, tm, d), row),
        pl.BlockSpec((1, 1, d), mod_map),
        pl.BlockSpec((1, 1, d), mod_map),
        pl.BlockSpec((1, d), vec_map),
        pl.BlockSpec(w.shape, vec_map),
    ]
    args = [x, shift, scale, g, w]
    k_spec = pl.BlockSpec((1, tm, KV_W), row)
    xr_spec = pl.BlockSpec((1, tm, D_RNN), row)
    k_shape = jax.ShapeDtypeStruct((bn, length, KV_W), BF16)
    xr_shape = jax.ShapeDtypeStruct((bn, length, D_RNN), F32)
    if latent:
        in_specs += [pl.BlockSpec((1, HEAD_DIM), vec_map), pl.BlockSpec((1, HEAD_DIM), vec_map),
                     pl.BlockSpec((tm, HEAD_DIM), lambda b, i: (i, 0)),
                     pl.BlockSpec((tm, HEAD_DIM), lambda b, i: (i, 0))]
        args += [qg, kg, cos, sin]
        out_specs = [pl.BlockSpec((1, tm, ATTN_W), row), k_spec, k_spec, xr_spec, xr_spec]
        out_shape = [jax.ShapeDtypeStruct((bn, length, ATTN_W), BF16), k_shape, k_shape, xr_shape, xr_shape]
    else:
        in_specs += [pl.BlockSpec((1, HEAD_DIM), vec_map)]
        args += [kg]
        out_specs = [k_spec, k_spec, xr_spec]
        out_shape = [k_shape, k_shape, xr_shape]

    return pl.pallas_call(
        functools.partial(_in_proj_kernel, latent=latent),
        grid=(bn, length // tm),
        in_specs=in_specs,
        out_specs=out_specs,
        out_shape=out_shape,
        scratch_shapes=[pltpu.VMEM((tm, d), BF16)],
        compiler_params=_params(),
        name="in_proj_latent" if latent else "in_proj_context",
    )(*args)


def _dwconv(x, w, b):
    n = x.shape[0]
    row = lax.broadcasted_iota(jnp.int32, x.shape, 0)
    y = b + x * w[2:3]
    y = y + jnp.where(row >= 2, pltpu.roll(x, 2, 0), 0.0) * w[0:1]
    y = y + jnp.where(row >= 1, pltpu.roll(x, 1, 0), 0.0) * w[1:2]
    y = y + jnp.where(row < n - 1, pltpu.roll(x, n - 1, 0), 0.0) * w[3:4]
    return y


def _scan_chunk(a, b, carry, reverse):
    tc = a.shape[0]
    nv = tc // SUBLANES
    a3 = a.reshape(nv, SUBLANES, LANES)
    b3 = b.reshape(nv, SUBLANES, LANES)
    row = lax.broadcasted_iota(jnp.int32, a3.shape, 1)
    step = 1
    while step < SUBLANES:
        if reverse:
            shift, valid = SUBLANES - step, row < SUBLANES - step
        else:
            shift, valid = step, row >= step
        a_sh = pltpu.roll(a3, shift, 1)
        b_sh = pltpu.roll(b3, shift, 1)
        b3 = jnp.where(valid, a3 * b_sh + b3, b3)
        a3 = jnp.where(valid, a3 * a_sh, a3)
        step *= 2
    last = 0 if reverse else SUBLANES - 1
    a_last = jnp.broadcast_to(a3[:, last:last + 1, :], a3.shape)
    b_last = jnp.broadcast_to(b3[:, last:last + 1, :], b3.shape)
    hs = [None] * nv
    for v in (range(nv - 1, -1, -1) if reverse else range(nv)):
        hs[v] = b3[v] + a3[v] * carry
        carry = b_last[v] + a_last[v] * carry
    return hs, carry


def _rglru_kernel(xl_ref, xc_ref, gl_ref, cw_ref, cb_ref, w_ref, bias_ref, lam_ref, o_ref,
                  xpad, xconv_c, xcs, hloc_f, aloc_f, hloc_b, aloc_b, hsum):
    cw = cw_ref[...]
    cb = cb_ref[...]
    length = xl_ref.shape[1]
    tseg = length // SUBLANES
    pitch = tseg + SEG_PAD
    n_j = tseg // SCAN_TJ
    n_ctx = xc_ref.shape[1] // SCAN_TC

    xconv_c[...] = _dwconv(xc_ref[0], cw, cb)

    pad_zeros = jnp.zeros((SEG_PAD, LANES), F32)
    xpad[0:SEG_PAD, :] = pad_zeros
    for s in range(SUBLANES):
        base = SEG_PAD + s * pitch
        xpad[base:base + tseg, :] = xl_ref[0, s * tseg:(s + 1) * tseg, :]
        pad = base + tseg
        xpad[pad:pad + SEG_PAD, :] = pad_zeros
        if s + 1 < SUBLANES:
            xpad[pad:pad + 1, :] = xl_ref[0, (s + 1) * tseg:(s + 1) * tseg + 1, :]
        xpad[pad + SEG_PAD - 2:pad + SEG_PAD, :] = xl_ref[0, (s + 1) * tseg - 2:(s + 1) * tseg, :]

    taps = [jnp.broadcast_to(cw[t:t + 1], (SUBLANES, LANES)) for t in range(CONV_W)]
    bias_rows = jnp.broadcast_to(cb, (SUBLANES, LANES))

    def conv_step(kk, _):
        j0 = kk * SCAN_TJ
        xs = [xpad[pl.ds(SEG_PAD + j0 - CONV_W // 2 + t, SUBLANES, stride=pitch), :]
              for t in range(SCAN_TJ + CONV_W - 1)]
        for jj in range(SCAN_TJ):
            y = bias_rows
            for t in range(CONV_W):
                y = y + xs[jj + t] * taps[t]
            xcs[pl.ds(pl.multiple_of((j0 + jj) * SUBLANES, SUBLANES), SUBLANES), :] = y
        return 0

    lax.fori_loop(0, n_j, conv_step, 0)

    def gates(xc, d):
        z = jnp.dot(xc.astype(BF16), w_ref[d, 0], preferred_element_type=F32) + bias_ref[d, 0]
        t_r = jnp.tanh(z[:, :RNN_BLOCK_W])
        t_i = jnp.tanh(z[:, RNN_BLOCK_W:])
        lam = lam_ref[d, 0]
        softplus_neg_lam = jnp.maximum(-lam, 0.0) + jnp.log1p(jnp.exp(-jnp.abs(lam)))
        half_rate = (-0.5 * RG_C) * softplus_neg_lam
        log_a = half_rate * t_r + half_rate
        a = jnp.exp(log_a)
        th = jnp.tanh(log_a)
        b = (jnp.sqrt(-2.0 * th / (1.0 - th)) * (0.5 * xc)) * (t_i + 1.0)
        return a, b

    ctx_state = []
    for d, reverse in ((0, False), (1, True)):
        def ctx_step(kk, carry, d=d, reverse=reverse):
            k = (n_ctx - 1 - kk) if reverse else kk
            r0 = pl.multiple_of(k * SCAN_TC, SCAN_TC)
            a, b = gates(xconv_c[pl.ds(r0, SCAN_TC), :], d)
            _, carry = _scan_chunk(a, b, carry, reverse)
            return carry

        ctx_state.append(lax.fori_loop(0, n_ctx, ctx_step, jnp.zeros((SUBLANES, LANES), F32)))

    def local_step(kk, carry):
        new = []
        for d, (h, acc_a) in enumerate(carry):
            reverse = d == 1
            j0 = ((n_j - 1 - kk) if reverse else kk) * SCAN_TJ
            src = pl.ds(pl.multiple_of(j0 * SUBLANES, SCAN_TJ * SUBLANES), SCAN_TJ * SUBLANES)
            a, b = gates(xcs[src, :], d)
            hloc, aloc = (hloc_b, aloc_b) if reverse else (hloc_f, aloc_f)
            for jj in (range(SCAN_TJ - 1, -1, -1) if reverse else range(SCAN_TJ)):
                rows = slice(jj * SUBLANES, (jj + 1) * SUBLANES)
                h = a[rows] * h + b[rows]
                acc_a = a[rows] * acc_a
                dst = pl.ds(pl.multiple_of((j0 + jj) * SUBLANES, SUBLANES), SUBLANES)
                hloc[dst, :] = h
                aloc[dst, :] = acc_a
            new.append((h, acc_a))
        return tuple(new)

    zeros = jnp.zeros((SUBLANES, LANES), F32)
    ones = jnp.ones((SUBLANES, LANES), F32)
    (h_f, a_f), (h_b, a_b) = lax.fori_loop(0, n_j, local_step, ((zeros, ones), (zeros, ones)))

    def entry_states(h_end, a_end, c_in, reverse):
        row = lax.broadcasted_iota(jnp.int32, (SUBLANES, LANES), 0)
        first = SUBLANES - 1 if reverse else 0
        c = c_in[0:1]
        out = jnp.where(row == first, jnp.broadcast_to(c, (SUBLANES, LANES)), 0.0)
        for k in range(1, SUBLANES):
            s = first - k if reverse else first + k
            prev = s + 1 if reverse else s - 1
            c = h_end[prev:prev + 1] + a_end[prev:prev + 1] * c
            out = jnp.where(row == s, jnp.broadcast_to(c, (SUBLANES, LANES)), out)
        return out

    c_f = entry_states(h_f, a_f, ctx_state[0], False)
    c_b = entry_states(h_b, a_b, ctx_state[1], True)

    def fix_step(kk, _):
        j0 = kk * SCAN_TJ
        for jj in range(SCAN_TJ):
            src = pl.ds(pl.multiple_of((j0 + jj) * SUBLANES, SUBLANES), SUBLANES)
            h = (hloc_f[src, :] + aloc_f[src, :] * c_f) + (hloc_b[src, :] + aloc_b[src, :] * c_b)
            hsum[pl.ds(j0 + jj, SUBLANES, stride=pitch), :] = h
        return 0

    lax.fori_loop(0, n_j, fix_step, 0)

    for s in range(SUBLANES):
        for r0 in range(0, tseg, SCAN_TC):
            src = slice(s * pitch + r0, s * pitch + r0 + SCAN_TC)
            dst = slice(s * tseg + r0, s * tseg + r0 + SCAN_TC)
            o_ref[0, dst, :] = (hsum[src, :] * _gelu(gl_ref[0, dst, :])).astype(BF16)


def _rglru(xl, xc, gl, conv_w, conv_b, w_gate, b_gate, lam):
    bn, s, _ = xl.shape
    ctx_len = xc.shape[1]
    bw = RNN_BLOCK_W
    padded = s + SUBLANES * SEG_PAD
    col = lambda b, n: (b, 0, n)
    return pl.pallas_call(
        _rglru_kernel,
        grid=(bn, RNN_BLOCKS),
        in_specs=[
            pl.BlockSpec((1, s, bw), col),
            pl.BlockSpec((1, ctx_len, bw), col),
            pl.BlockSpec((1, s, bw), col),
            pl.BlockSpec((CONV_W, bw), lambda b, n: (0, n)),
            pl.BlockSpec((1, bw), lambda b, n: (0, n)),
            pl.BlockSpec((2, 1, bw, 2 * bw), lambda b, n: (0, n, 0, 0)),
            pl.BlockSpec((2, 1, 1, 2 * bw), lambda b, n: (0, n, 0, 0)),
            pl.BlockSpec((2, 1, 1, bw), lambda b, n: (0, n, 0, 0)),
        ],
        out_specs=pl.BlockSpec((1, s, bw), col),
        out_shape=jax.ShapeDtypeStruct((bn, s, D_RNN), BF16),
        scratch_shapes=[
            pltpu.VMEM((SEG_PAD + padded, bw), F32),
            pltpu.VMEM((ctx_len, bw), F32),
            pltpu.VMEM((s, bw), F32),
            pltpu.VMEM((s, bw), F32), pltpu.VMEM((s, bw), F32),
            pltpu.VMEM((s, bw), F32), pltpu.VMEM((s, bw), F32),
            pltpu.VMEM((padded, bw), F32),
        ],
        compiler_params=_params(),
        name="rglru_bidirectional",
    )(xl, xc, gl, conv_w, conv_b, w_gate, b_gate, lam)


def _attn_kernel(q_ref, kc_ref, kl_ref, vc_ref, vl_ref, o_ref, k_scr, v_scr):
    n_ctx = kc_ref.shape[1]

    @pl.when(pl.program_id(2) == 0)
    def _():
        k_scr[0:n_ctx, :] = kc_ref[0]
        k_scr[n_ctx:, :] = kl_ref[0]
        v_scr[0:n_ctx, 0:HEAD_DIM] = vc_ref[0]
        v_scr[n_ctx:, 0:HEAD_DIM] = vl_ref[0]
        v_scr[:, HEAD_DIM:] = jnp.ones((v_scr.shape[0], HEAD_DIM), BF16)

    k = k_scr[...]
    v = v_scr[...]
    c = (HEAD_DIM ** -0.5) * 1.4426950408889634
    units = [(slice(r0, r0 + ATTN_SUB), slice(hh * HEAD_DIM, (hh + 1) * HEAD_DIM))
             for r0 in range(0, q_ref.shape[1], ATTN_SUB) for hh in range(N_HEADS // N_KV_HEADS)]

    def scores(u):
        rows, sl = units[u]
        return lax.dot_general(q_ref[0, rows, sl], k, (((1,), (1,)), ((), ())), preferred_element_type=F32)

    s_next = scores(0)
    for u, (rows, sl) in enumerate(units):
        s = s_next
        if u + 1 < len(units):
            s_next = scores(u + 1)
        m = jnp.max(s, axis=-1, keepdims=True)
        p = jnp.exp2((s - m) * c).astype(BF16)
        ov = jnp.dot(p, v, preferred_element_type=F32)
        o_ref[0, rows, sl] = (ov[:, :HEAD_DIM] / ov[:, HEAD_DIM:HEAD_DIM + 1]).astype(BF16)


def _attention(q, kc, kl, vc, vl):
    bn, s, _ = q.shape
    n_ctx, n_lat = kc.shape[1], kl.shape[1]
    t = n_ctx + n_lat
    gw = ATTN_W // N_KV_HEADS
    kv_spec = lambda n: pl.BlockSpec((1, n, HEAD_DIM), lambda b, h, i: (b, 0, h))
    return pl.pallas_call(
        _attn_kernel,
        grid=(bn, N_KV_HEADS, s // ATTN_TQ),
        in_specs=[
            pl.BlockSpec((1, ATTN_TQ, gw), lambda b, h, i: (b, i, h)),
            kv_spec(n_ctx), kv_spec(n_lat), kv_spec(n_ctx), kv_spec(n_lat),
        ],
        out_specs=pl.BlockSpec((1, ATTN_TQ, gw), lambda b, h, i: (b, i, h)),
        out_shape=jax.ShapeDtypeStruct((bn, s, ATTN_W), BF16),
        scratch_shapes=[pltpu.VMEM((t, HEAD_DIM), BF16), pltpu.VMEM((t, 2 * HEAD_DIM), BF16)],
        compiler_params=_params(),
        name="gqa_attention",
    )(q, kc, kl, vc, vl)


def _residual_and_next(x, mix, gate, g_post, g_next, shift_next, scale_next):
    x1 = x + _rms(mix, gate * g_post)
    h = _rms(x1, g_next * (1.0 + scale_next)) + shift_next
    return x1, h.astype(BF16)


def _out_proj_kernel(a1_ref, a2_ref, w_ref, x_ref, gate_ref, g1_ref, g2_ref, sh_ref, sc_ref, x1_ref, h_ref):
    ka = a1_ref.shape[2]
    for r0 in range(0, x_ref.shape[1], EPI_SUB):
        rows = slice(r0, r0 + EPI_SUB)
        mix = jnp.dot(a1_ref[0, rows, :], w_ref[0:ka, :], preferred_element_type=F32)
        mix = mix + jnp.dot(a2_ref[0, rows, :], w_ref[ka:, :], preferred_element_type=F32)
        x1, h = _residual_and_next(x_ref[0, rows, :], mix, gate_ref[0], g1_ref[...], g2_ref[...],
                                   sh_ref[0], sc_ref[0])
        x1_ref[0, rows, :] = x1
        h_ref[0, rows, :] = h


def _out_proj(a1, a2, w, x, gate, g1, g2, shift, scale):
    bn, s, d = x.shape
    tm = OUT_TM
    row = lambda b, i: (b, i, 0)
    mod = lambda b, i: (b, 0, 0)
    vec = lambda b, i: (0, 0)
    return pl.pallas_call(
        _out_proj_kernel,
        grid=(bn, s // tm),
        in_specs=[
            pl.BlockSpec((1, tm, a1.shape[2]), row),
            pl.BlockSpec((1, tm, a2.shape[2]), row),
            pl.BlockSpec(w.shape, vec, pipeline_mode=pl.Buffered(1)),
            pl.BlockSpec((1, tm, d), row),
            pl.BlockSpec((1, 1, d), mod),
            pl.BlockSpec((1, d), vec),
            pl.BlockSpec((1, d), vec),
            pl.BlockSpec((1, 1, d), mod),
            pl.BlockSpec((1, 1, d), mod),
        ],
        out_specs=[pl.BlockSpec((1, tm, d), row), pl.BlockSpec((1, tm, d), row)],
        out_shape=[jax.ShapeDtypeStruct((bn, s, d), F32), jax.ShapeDtypeStruct((bn, s, d), BF16)],
        compiler_params=pltpu.CompilerParams(vmem_limit_bytes=BIG_VMEM_LIMIT_BYTES),
        name="out_proj_residual",
    )(a1, a2, w, x, gate, g1, g2, shift, scale)


def _ffn_kernel(*refs, has_next, na):
    if has_next:
        (h_ref, w1_ref, w2_ref, x_ref, gate_ref, g3_ref, gn_ref, sh_ref, sc_ref,
         x2_ref, hn_ref, t_scr) = refs
    else:
        h_ref, w1_ref, w2_ref, x_ref, gate_ref, g3_ref, x2_ref, t_scr = refs
    j = pl.program_id(2)
    ta = t_scr.shape[2]
    tb = w2_ref.shape[1]

    @pl.when(j < na)
    def _():
        t = jnp.dot(h_ref[0], w1_ref[...], preferred_element_type=F32)
        t = jnp.maximum(t, 0.0)
        t_scr[j] = (t * t).astype(BF16)

    last = pl.num_programs(2) - 1

    def down_proj(rows):
        y = jnp.dot(t_scr[0, rows, :], w2_ref[0:ta, :], preferred_element_type=F32)
        for a in range(1, na):
            y = y + jnp.dot(t_scr[a, rows, :], w2_ref[a * ta:(a + 1) * ta, :], preferred_element_type=F32)
        return y

    @pl.when(jnp.logical_and(j >= na, j < last))
    def _():
        col = pl.multiple_of((j - na) * tb, tb)
        x2_ref[0, :, pl.ds(col, tb)] = down_proj(slice(None))

    @pl.when(j == last)
    def _():
        col0 = x2_ref.shape[2] - tb
        for r0 in range(0, x2_ref.shape[1], EPI_SUB):
            rows = slice(r0, r0 + EPI_SUB)
            x2_ref[0, rows, col0:] = down_proj(rows)
            if has_next:
                x2, hn = _residual_and_next(x_ref[0, rows, :], x2_ref[0, rows, :], gate_ref[0], g3_ref[...],
                                            gn_ref[...], sh_ref[0], sc_ref[0])
                x2_ref[0, rows, :] = x2
                hn_ref[0, rows, :] = hn
            else:
                x2_ref[0, rows, :] = x_ref[0, rows, :] + _rms(x2_ref[0, rows, :], gate_ref[0] * g3_ref[...])


def _ffn(h, w1, w2, layer, x, gate, g3, nxt=None):
    bn, s, d = x.shape
    tm = FFN_TM
    na, ta = w1.shape[1], w1.shape[3]
    nb, tb = w2.shape[1], w2.shape[3]
    dff = na * ta
    row = lambda b, i, j: (b, i, 0)
    mod = lambda b, i, j: (b, 0, 0)
    vec = lambda b, i, j: (0, 0)
    in_specs = [
        pl.BlockSpec((1, tm, d), row),
        pl.BlockSpec((None, None, d, ta), lambda b, i, j: (layer, jnp.minimum(j, na - 1), 0, 0)),
        pl.BlockSpec((None, None, dff, tb), lambda b, i, j: (layer, jnp.maximum(j - na, 0), 0, 0)),
        pl.BlockSpec((1, tm, d), row),
        pl.BlockSpec((1, 1, d), mod),
        pl.BlockSpec((1, d), vec),
    ]
    args = [h, w1, w2, x, gate, g3]
    out_specs = [pl.BlockSpec((1, tm, d), row)]
    out_shape = [jax.ShapeDtypeStruct((bn, s, d), F32)]
    if nxt is not None:
        in_specs += [pl.BlockSpec((1, d), vec), pl.BlockSpec((1, 1, d), mod), pl.BlockSpec((1, 1, d), mod)]
        args += list(nxt)
        out_specs.append(pl.BlockSpec((1, tm, d), row))
        out_shape.append(jax.ShapeDtypeStruct((bn, s, d), BF16))
    return pl.pallas_call(
        functools.partial(_ffn_kernel, has_next=nxt is not None, na=na),
        grid=(bn, s // tm, na + nb),
        in_specs=in_specs,
        out_specs=out_specs,
        out_shape=out_shape,
        scratch_shapes=[pltpu.VMEM((na, tm, ta), BF16)],
        compiler_params=pltpu.CompilerParams(vmem_limit_bytes=BIG_VMEM_LIMIT_BYTES),
        name="sq_relu_mlp",
    )(*args)


def _gm_in_kernel(h_ref, w_ref, b_ref, z_ref):
    c0 = 0
    for width in GM_GROUP_COLS:
        cols = slice(c0, c0 + width)
        z = jnp.dot(h_ref[0], w_ref[:, cols], preferred_element_type=F32) + b_ref[:, cols]
        z_ref[0, :, cols] = _gelu(z)
        c0 += width


def _gm_in(h, w, b):
    bn, s, d = h.shape
    n = w.shape[1]
    return pl.pallas_call(
        _gm_in_kernel,
        grid=(n // GM_TN, bn, s // GM_TM),
        in_specs=[
            pl.BlockSpec((1, GM_TM, d), lambda j, b_, i: (b_, i, 0)),
            pl.BlockSpec((d, GM_TN), lambda j, b_, i: (0, j)),
            pl.BlockSpec((1, GM_TN), lambda j, b_, i: (0, j)),
        ],
        out_specs=pl.BlockSpec((1, GM_TM, GM_TN), lambda j, b_, i: (b_, i, j)),
        out_shape=jax.ShapeDtypeStruct((bn, s, n), F32),
        compiler_params=_params(),
        name="gmlp_in_proj",
    )(h, w, b)


def _gm_out_kernel(u_ref, v_ref, vg_ref, vb_ref, wsp_ref, bsp_ref, w_ref, x_ref, gate_ref, g1_ref, g2_ref,
                   sh_ref, sc_ref, x1_ref, h_ref, vn_scr, gated_scr):
    for r0 in range(0, x_ref.shape[1], EPI_SUB):
        sub = slice(r0, r0 + EPI_SUB)
        v = v_ref[0, sub, :]
        mu = jnp.mean(v, axis=-1, keepdims=True)
        vc = v - mu
        var = jnp.mean(vc * vc, axis=-1, keepdims=True)
        vn_scr[sub, :] = (vc * lax.rsqrt(var + EPS) * vg_ref[...] + vb_ref[...]).astype(BF16)
        for c0 in range(r0, r0 + EPI_SUB, CHUNK):
            rows = slice(c0, c0 + CHUNK)
            for g in range(GM_GROUPS):
                cols = slice(g * GM_GROUP_W, (g + 1) * GM_GROUP_W)
                sv = jnp.dot(wsp_ref[g], vn_scr[rows, cols], preferred_element_type=F32) + bsp_ref[:, g:g + 1]
                gated_scr[rows, cols] = (u_ref[0, rows, cols] * sv).astype(BF16)
        mix = jnp.dot(gated_scr[sub, :], w_ref[...], preferred_element_type=F32)
        x1, h = _residual_and_next(x_ref[0, sub, :], mix, gate_ref[0], g1_ref[...], g2_ref[...],
                                   sh_ref[0], sc_ref[0])
        x1_ref[0, sub, :] = x1
        h_ref[0, sub, :] = h


def _gm_out(z, vg, vb, wsp, bsp_t, w, x, gate, g1, g2, shift, scale):
    bn, s, d = x.shape
    dg = z.shape[2] // 2
    tm = GMO_TM
    row = lambda b, i: (b, i, 0)
    mod = lambda b, i: (b, 0, 0)
    vec = lambda b, i: (0, 0)
    return pl.pallas_call(
        _gm_out_kernel,
        grid=(bn, s // tm),
        in_specs=[
            pl.BlockSpec((1, tm, dg), lambda b, i: (b, i, 0)),
            pl.BlockSpec((1, tm, dg), lambda b, i: (b, i, 1)),
            pl.BlockSpec((1, dg), vec),
            pl.BlockSpec((1, dg), vec),
            pl.BlockSpec(wsp.shape, lambda b, i: (0, 0, 0), pipeline_mode=pl.Buffered(1)),
            pl.BlockSpec(bsp_t.shape, vec),
            pl.BlockSpec(w.shape, vec, pipeline_mode=pl.Buffered(1)),
            pl.BlockSpec((1, tm, d), row),
            pl.BlockSpec((1, 1, d), mod),
            pl.BlockSpec((1, d), vec),
            pl.BlockSpec((1, d), vec),
            pl.BlockSpec((1, 1, d), mod),
            pl.BlockSpec((1, 1, d), mod),
        ],
        out_specs=[pl.BlockSpec((1, tm, d), row), pl.BlockSpec((1, tm, d), row)],
        out_shape=[jax.ShapeDtypeStruct((bn, s, d), F32), jax.ShapeDtypeStruct((bn, s, d), BF16)],
        scratch_shapes=[pltpu.VMEM((tm, dg), BF16), pltpu.VMEM((tm, dg), BF16)],
        compiler_params=_params(),
        name="gmlp_spatial_out_proj",
    )(z, z, vg, vb, wsp, bsp_t, w, x, gate, g1, g2, shift, scale)


def _rope_tables(n):
    t = np.arange(n)
    r_idx = (t // GRID_W).astype(np.float32)
    c_idx = (t % GRID_W).astype(np.float32)
    freqs = np.float32(ROPE_THETA) ** (-np.arange(ROPE_PAIRS, dtype=np.float32) / np.float32(ROPE_PAIRS))
    ang_r = r_idx[:, None] * freqs
    ang_c = c_idx[:, None] * freqs
    cos = np.concatenate([np.cos(ang_r), np.cos(ang_r), np.cos(ang_c), np.cos(ang_c)], axis=-1)
    sin = np.concatenate([-np.sin(ang_r), np.sin(ang_r), -np.sin(ang_c), np.sin(ang_c)], axis=-1)
    return jnp.asarray(cos, F32), jnp.asarray(sin, F32)


def kernel(x, c, ctx, c_ctx, w_mod, b_mod, norm_g, w_ff_in, w_ff_out, ar_w_in, ar_q_g, ar_k_g, ar_conv_w,
           ar_conv_b, ar_wa, ar_ba, ar_wx, ar_bx, ar_lambda, ar_w_out, gm_w_in, gm_b_in, gm_v_g, gm_v_b,
           gm_w_sp, gm_b_sp, gm_w_out):
    bn, s, d = x.shape
    depth = w_mod.shape[0]
    assert depth == 2, "layer pattern implemented for one attention/recurrent layer followed by one gMLP layer"

    rows = -(-(bn + 1) // SUBLANES) * SUBLANES
    cc = jnp.concatenate([c, c_ctx[None, :], jnp.zeros((rows - bn - 1, d), F32)], axis=0)
    mods = _modulation(cc, w_mod, b_mod)

    def lat_mod(layer, k):
        return mods[layer, :bn, k * d:(k + 1) * d].reshape(bn, 1, d)

    def ctx_mod(layer, k):
        return mods[layer, bn:bn + 1, k * d:(k + 1) * d].reshape(1, 1, d)

    g = norm_g.reshape(depth, 4, 1, d)
    w_ff1 = _to_bf16_col_blocks(w_ff_in, FFN_TA)
    w_ff2 = _to_bf16_col_blocks(w_ff_out, FFN_TB)

    w_in = ar_w_in[0].astype(BF16)
    qg = ar_q_g[0].reshape(1, HEAD_DIM)
    kg = ar_k_g[0].reshape(1, HEAD_DIM)
    cos, sin = _rope_tables(s)
    ql, kl, vl, xrl, grl = _in_proj(x, lat_mod(0, 0), lat_mod(0, 1), g[0, 0], w_in, qg, kg, cos, sin, latent=True)
    kc, vc, xrc = _in_proj(ctx, ctx_mod(0, 0), ctx_mod(0, 1), g[0, 0], w_in, qg, kg, None, None, latent=False)

    attn = _attention(ql, kc, kl, vc, vl)

    w_gate = (0.5 * jnp.concatenate([ar_wa[0], ar_wx[0]], axis=-1)).astype(BF16)
    b_gate = 0.5 * jnp.concatenate([ar_ba[0].reshape(2, RNN_BLOCKS, 1, RNN_BLOCK_W),
                                    ar_bx[0].reshape(2, RNN_BLOCKS, 1, RNN_BLOCK_W)], axis=-1)
    lam = ar_lambda[0].reshape(2, RNN_BLOCKS, 1, RNN_BLOCK_W)
    rnn = _rglru(xrl, xrc, grl, ar_conv_w[0], ar_conv_b[0].reshape(1, D_RNN), w_gate, b_gate, lam)

    x1, h = _out_proj(attn, rnn, ar_w_out[0].astype(BF16), x, lat_mod(0, 2), g[0, 1], g[0, 2],
                      lat_mod(0, 3), lat_mod(0, 4))
    x2, h = _ffn(h, w_ff1, w_ff2, 0, x1, lat_mod(0, 5), g[0, 3], nxt=(g[1, 0], lat_mod(1, 0), lat_mod(1, 1)))

    z = _gm_in(h, gm_w_in[0].astype(BF16), gm_b_in[0].reshape(1, -1))
    x3, h = _gm_out(z, gm_v_g[0].reshape(1, -1), gm_v_b[0].reshape(1, -1), gm_w_sp[0].astype(BF16),
                    gm_b_sp[0].T, gm_w_out[0].astype(BF16), x2, lat_mod(1, 2), g[1, 1], g[1, 2],
                    lat_mod(1, 3), lat_mod(1, 4))
    (x4,) = _ffn(h, w_ff1, w_ff2, 1, x3, lat_mod(1, 5), g[1, 3])
    return x4
```
